```python
import math
import jax
import jax.numpy as jnp
from jax import lax
import numpy as np

D_MODEL = 2048
BATCH = 4
SEQ = 4096
DEPTH = 2

CTX_LEN = 256
GRID_W = 64
HEAD_DIM = 128
N_HEADS_TOTAL = D_MODEL // HEAD_DIM
A_HEADS = N_HEADS_TOTAL // 4
A_QK_DIM = HEAD_DIM // 2
B_HEADS = N_HEADS_TOTAL // 2
B_KV_HEADS = B_HEADS // 4
C_HEADS = N_HEADS_TOTAL // 4
NA_ROWS_MAX = 8
NA_COLS = 16
Q_BLOCK = 128
ROPE_THETA = 10000.0
EPS = 1e-6
D_FF = ((8 * D_MODEL + 3 * 256 - 1) // (3 * 256)) * 256
A_QK = A_HEADS * 2 * A_QK_DIM
A_V = A_HEADS * HEAD_DIM
B_Q = B_HEADS * HEAD_DIM
B_KV = B_KV_HEADS * HEAD_DIM
C_W = C_HEADS * HEAD_DIM
IN_SIZES = (A_QK, A_QK, A_V, B_Q, B_KV, B_KV, C_W, C_W, C_W)
IN_COLS = sum(IN_SIZES)
MIX_OUT = A_V + B_Q + C_W

kernel_name = 'hybrid_diffusion_parallel_heads'


def _rms(x, g):
    xf = x.astype(jnp.float32)
    y = xf * lax.rsqrt(jnp.mean(xf * xf, axis=-1, keepdims=True) + EPS)
    return (y * g.astype(jnp.float32)).astype(x.dtype)


def _rope_axis(xp, pos):
    quarter = xp.shape[-1] // 2
    inv = ROPE_THETA ** (-jnp.arange(quarter, dtype=jnp.float32) / quarter)
    ang = pos.astype(jnp.float32)[:, None] * inv[None, :]
    cos, sin = jnp.cos(ang), jnp.sin(ang)
    x1, x2 = xp[..., :quarter], xp[..., quarter:]
    return jnp.concatenate([x1 * cos - x2 * sin, x1 * sin + x2 * cos], axis=-1)


def axial_rope(x, row, col):
    half = x.shape[-1] // 2
    xf = x.astype(jnp.float32)
    return jnp.concatenate([_rope_axis(xf[..., :half], row),
                            _rope_axis(xf[..., half:], col)], axis=-1).astype(x.dtype)


def split_cols(p):
    outs, start = [], 0
    for n in IN_SIZES:
        outs.append(p[..., start:start + n])
        start += n
    return outs


def to_heads(t, h):
    b, n, _ = t.shape
    return t.reshape(b, n, h, -1).transpose(0, 2, 1, 3)


def from_heads(t):
    b, h, n, d = t.shape
    return t.transpose(0, 2, 1, 3).reshape(b, n, h * d)


def sweep_query_blocks(fn, *qs):
    s = qs[0].shape[-2]
    nb = s // Q_BLOCK
    blocks = tuple(jnp.moveaxis(q.reshape(q.shape[:-2] + (nb, Q_BLOCK, q.shape[-1])), -3, 0)
                   for q in qs)
    out = lax.map(lambda bl: fn(*bl), blocks)
    out = jnp.moveaxis(out, 0, -3)
    return out.reshape(out.shape[:-3] + (s, out.shape[-1]))


def dense_attention(q, k, v):
    s = jnp.einsum('bhqd,bhtd->bhqt', q, k).astype(jnp.float32) * (q.shape[-1] ** -0.5)
    p = jax.nn.softmax(s, axis=-1).astype(v.dtype)
    return jnp.einsum('bhqt,bhtd->bhqd', p, v)


def diff_attention(q1, q2, k1, k2, v, lam):
    scale = A_QK_DIM ** -0.5
    s1 = jnp.einsum('bhqd,bhtd->bhqt', q1, k1).astype(jnp.float32) * scale
    s2 = jnp.einsum('bhqd,bhtd->bhqt', q2, k2).astype(jnp.float32) * scale
    p = jax.nn.softmax(s1, axis=-1) - lam * jax.nn.softmax(s2, axis=-1)
    return jnp.einsum('bhqt,bhtd->bhqd', p.astype(v.dtype), v)


def diff_mixer(q, k, v, qc, kc, vc, row, col, lq1, lk1, lq2, lk2, g_sub, layer_idx, need_ctx):
    lam_init = 0.8 - 0.6 * math.exp(-0.3 * layer_idx)
    f32 = jnp.float32
    lam = (jnp.exp(jnp.sum(lq1.astype(f32) * lk1.astype(f32)))
           - jnp.exp(jnp.sum(lq2.astype(f32) * lk2.astype(f32))) + lam_init)

    def qk_pair(t):
        b, n, _ = t.shape
        t = t.reshape(b, n, A_HEADS, 2, A_QK_DIM).transpose(3, 0, 2, 1, 4)
        return t[0], t[1]

    q1, q2 = qk_pair(q)
    k1, k2 = qk_pair(k)
    q1, q2, k1, k2 = (axial_rope(t, row, col) for t in (q1, q2, k1, k2))
    k1c, k2c = qk_pair(kc)
    vch = to_heads(vc, A_HEADS)
    k1a = jnp.concatenate([k1c, k1], axis=2)
    k2a = jnp.concatenate([k2c, k2], axis=2)
    va = jnp.concatenate([vch, to_heads(v, A_HEADS)], axis=2)
    o = sweep_query_blocks(lambda a, b2: diff_attention(a, b2, k1a, k2a, va, lam), q1, q2)
    out = from_heads(_rms(o, g_sub) * (1.0 - lam_init))
    out_c = None
    if need_ctx:
        q1c, q2c = qk_pair(qc)
        oc = diff_attention(q1c, q2c, k1c, k2c, vch, lam)
        out_c = from_heads(_rms(oc, g_sub) * (1.0 - lam_init))
    return out, out_c


def gqa_attention(q, k, v):
    s = jnp.einsum('bkgqd,bktd->bkgqt', q, k).astype(jnp.float32) * (HEAD_DIM ** -0.5)
    p = jax.nn.softmax(s, axis=-1).astype(v.dtype)
    return jnp.einsum('bkgqt,bktd->bkgqd', p, v)


def gqa_mixer(q, k, v, qc, kc, vc, row, col, g_qn, g_kn, need_ctx):
    group = B_HEADS // B_KV_HEADS

    def prep_q(t, rope):
        b, n, _ = t.shape
        t = _rms(t.reshape(b, n, B_HEADS, HEAD_DIM), g_qn).transpose(0, 2, 1, 3)
        if rope:
            t = axial_rope(t, row, col)
        return t.reshape(b, B_KV_HEADS, group, n, HEAD_DIM)

    def prep_k(t, rope):
        b, n, _ = t.shape
        t = _rms(t.reshape(b, n, B_KV_HEADS, HEAD_DIM), g_kn).transpose(0, 2, 1, 3)
        if rope:
            t = axial_rope(t, row, col)
        return t

    kch, vch = prep_k(kc, False), to_heads(vc, B_KV_HEADS)
    ka = jnp.concatenate([kch, prep_k(k, True)], axis=2)
    va = jnp.concatenate([vch, to_heads(v, B_KV_HEADS)], axis=2)
    o = sweep_query_blocks(lambda qb: gqa_attention(qb, ka, va), prep_q(q, True))
    b, _, _, n, _ = o.shape
    out = from_heads(o.reshape(b, B_HEADS, n, HEAD_DIM))
    out_c = None
    if need_ctx:
        oc = gqa_attention(prep_q(qc, False), kch, vch)
        out_c = from_heads(oc.reshape(oc.shape[0], B_HEADS, oc.shape[3], HEAD_DIM))
    return out, out_c


def na_mixer(q, k, v, qc, kc, vc, rpb_l, need_ctx):
    qh, kh, vh = to_heads(q, C_HEADS), to_heads(k, C_HEADS), to_heads(v, C_HEADS)
    kch, vch = to_heads(kc, C_HEADS), to_heads(vc, C_HEADS)
    b, hh, s, d = qh.shape
    rows_n = s // GRID_W
    wr = min(NA_ROWS_MAX, rows_n)
    scale = d ** -0.5
    grid = lambda t: t.reshape(b, hh, rows_n, GRID_W, d)
    qg, kg, vg = grid(qh), grid(kh), grid(vh)
    r = jnp.arange(rows_n)
    r0 = jnp.clip(r - wr // 2, 0, rows_n - wr)
    row_idx = r0[:, None] + jnp.arange(wr)[None, :]
    k_band = kg[:, :, row_idx]
    v_band = vg[:, :, row_idx]
    cidx = jnp.arange(GRID_W)
    c0 = jnp.clip(cidx - NA_COLS // 2, 0, GRID_W - NA_COLS)
    col_ok = (cidx[None, :] >= c0[:, None]) & (cidx[None, :] < c0[:, None] + NA_COLS)
    roff = row_idx - r[:, None] + NA_ROWS_MAX - 1
    coff = jnp.clip(cidx[None, :] - cidx[:, None], -(NA_COLS - 1), NA_COLS - 1) + NA_COLS - 1
    bias = rpb_l[:, roff[:, None, :, None], coff[None, :, None, :]]
    s_win = (jnp.einsum('bhrqd,bhrikd->bhrqik', qg, k_band).astype(jnp.float32) * scale
             + bias.astype(jnp.float32)[None])
    s_win = jnp.where(col_ok[:, None, :], s_win, -jnp.inf)
    s_ctx = jnp.einsum('bhrqd,bhtd->bhrqt', qg, kch).astype(jnp.float32) * scale
    nwin = wr * GRID_W
    p = jax.nn.softmax(jnp.concatenate(
        [s_win.reshape(b, hh, rows_n, GRID_W, nwin), s_ctx], axis=-1), axis=-1).astype(vh.dtype)
    o = (jnp.einsum('bhrqik,bhrikd->bhrqd',
                    p[..., :nwin].reshape(b, hh, rows_n, GRID_W, wr, GRID_W), v_band)
         + jnp.einsum('bhrqt,bhtd->bhrqd', p[..., nwin:], vch))
    out = from_heads(o.reshape(b, hh, s, d))
    out_c = None
    if need_ctx:
        out_c = from_heads(dense_attention(to_heads(qc, C_HEADS), kch, vch))
    return out, out_c


def modulate(h, g, shift, scale):
    return _rms(h, g) * (1.0 + scale) + shift


def swiglu(f, wg, wu, wd):
    return (jax.nn.silu(f @ wg) * (f @ wu)) @ wd


def setup_inputs(seed: int = 0) -> dict:
    key = jax.random.key(seed)
    ks = jax.random.split(key, 23)
    f32 = jnp.float32

    def nrm(k, shape, scale):
        return jax.random.normal(k, shape, f32) * scale

    def gain(k, shape):
        return 1.0 + 0.05 * jax.random.normal(k, shape, f32)

    return {
        'x': nrm(ks[0], (BATCH, SEQ, D_MODEL), 1.0),
        'c': nrm(ks[1], (BATCH, D_MODEL), 1.0),
        'ctx': nrm(ks[2], (BATCH, CTX_LEN, D_MODEL), 1.0),
        'c_ctx': nrm(ks[3], (D_MODEL,), 1.0),
        'w_mod': nrm(ks[4], (DEPTH, D_MODEL, 6 * D_MODEL), 0.5 * D_MODEL ** -0.5),
        'b_mod': nrm(ks[5], (DEPTH, 6 * D_MODEL), 0.01),
        'g_pre1': gain(ks[6], (DEPTH, D_MODEL)),
        'g_post1': gain(ks[7], (DEPTH, D_MODEL)),
        'g_pre2': gain(ks[8], (DEPTH, D_MODEL)),
        'g_post2': gain(ks[9], (DEPTH, D_MODEL)),
        'w_in': nrm(ks[10], (DEPTH, D_MODEL, IN_COLS), D_MODEL ** -0.5),
        'w_out': nrm(ks[11], (DEPTH, MIX_OUT, D_MODEL), MIX_OUT ** -0.5),
        'lam_q1': nrm(ks[12], (DEPTH, A_QK_DIM), 0.1),
        'lam_k1': nrm(ks[13], (DEPTH, A_QK_DIM), 0.1),
        'lam_q2': nrm(ks[14], (DEPTH, A_QK_DIM), 0.1),
        'lam_k2': nrm(ks[15], (DEPTH, A_QK_DIM), 0.1),
        'g_diff': gain(ks[16], (DEPTH, HEAD_DIM)),
        'g_qn': gain(ks[17], (DEPTH, HEAD_DIM)),
        'g_kn': gain(ks[18], (DEPTH, HEAD_DIM)),
        'rpb': nrm(ks[19], (DEPTH, C_HEADS, 2 * NA_ROWS_MAX - 1, 2 * NA_COLS - 1), 0.02),
        'w_gate': nrm(ks[20], (DEPTH, D_MODEL, D_FF), D_MODEL ** -0.5),
        'w_up': nrm(ks[21], (DEPTH, D_MODEL, D_FF), D_MODEL ** -0.5),
        'w_down': nrm(ks[22], (DEPTH, D_FF, D_MODEL), D_FF ** -0.5),
    }


def reference(x, c, ctx, c_ctx, w_mod, b_mod, g_pre1, g_post1, g_pre2, g_post2,
              w_in, w_out, lam_q1, lam_k1, lam_q2, lam_k2, g_diff, g_qn, g_kn, rpb,
              w_gate, w_up, w_down):
    s = x.shape[1]
    t = jnp.arange(s)
    row, col = t // GRID_W, t % GRID_W
    cond = jax.nn.silu(c)[:, None, :]
    cond_c = jax.nn.silu(c_ctx)[None, None, :]
    h, hc = x, ctx
    for l in range(DEPTH):
        need_ctx = l < DEPTH - 1
        sh1, sc1, gt1, sh2, sc2, gt2 = jnp.split(cond @ w_mod[l] + b_mod[l], 6, axis=-1)
        sh1c, sc1c, gt1c, sh2c, sc2c, gt2c = jnp.split(cond_c @ w_mod[l] + b_mod[l], 6, axis=-1)
        qa, ka, va, qb, kb, vb, qn, kn, vn = split_cols(modulate(h, g_pre1[l], sh1, sc1) @ w_in[l])
        qac, kac, vac, qbc, kbc, vbc, qnc, knc, vnc = split_cols(
            modulate(hc, g_pre1[l], sh1c, sc1c) @ w_in[l])
        oa, oac = diff_mixer(qa, ka, va, qac, kac, vac, row, col, lam_q1[l], lam_k1[l],
                             lam_q2[l], lam_k2[l], g_diff[l], l, need_ctx)
        ob, obc = gqa_mixer(qb, kb, vb, qbc, kbc, vbc, row, col, g_qn[l], g_kn[l], need_ctx)
        on, onc = na_mixer(qn, kn, vn, qnc, knc, vnc, rpb[l], need_ctx)
        h = h + gt1 * _rms(jnp.concatenate([oa, ob, on], axis=-1) @ w_out[l], g_post1[l])
        h = h + gt2 * _rms(swiglu(modulate(h, g_pre2[l], sh2, sc2), w_gate[l], w_up[l], w_down[l]),
                           g_post2[l])
        if need_ctx:
            hc = hc + gt1c * _rms(jnp.concatenate([oac, obc, onc], axis=-1) @ w_out[l], g_post1[l])
            hc = hc + gt2c * _rms(swiglu(modulate(hc, g_pre2[l], sh2c, sc2c),
                                         w_gate[l], w_up[l], w_down[l]), g_post2[l])
    return h
```

```python
import functools
import math

import numpy as np
import jax
import jax.numpy as jnp
from jax import lax
from jax.experimental import pallas as pl
from jax.experimental.pallas import tpu as pltpu

GRID_W = 64
HEAD_DIM = 128
NA_COLS = 16
ROPE_THETA = 10000.0
EPS = 1e-6
NEG_BIG = -1e30
MOD_ROWS = 8
V7X_VMEM_LIMIT = 56 * 1024 * 1024

F32 = jnp.float32
BF16 = jnp.bfloat16


def _cparams(sem):
    return pltpu.CompilerParams(dimension_semantics=sem, vmem_limit_bytes=V7X_VMEM_LIMIT)


def _pick(n, candidates):
    for c in candidates:
        if n % c == 0:
            return c
    raise ValueError(f"no tile in {candidates} divides {n}")


def _mod_kernel(c_ref, w_ref, b_ref, o_ref):
    c = c_ref[...]
    a = (c * jax.nn.sigmoid(c)).astype(BF16)
    o_ref[0] = jnp.dot(a, w_ref[0].astype(BF16), preferred_element_type=F32) + b_ref[0]


def _mod_vectors(c_rows, w_mod, b_mod):
    depth, d, n = w_mod.shape
    tn = _pick(n, (1024, 512, 256, 128))
    return pl.pallas_call(
        _mod_kernel,
        grid=(depth, n // tn),
        in_specs=[
            pl.BlockSpec((MOD_ROWS, d), lambda l, j: (0, 0)),
            pl.BlockSpec((1, d, tn), lambda l, j: (l, 0, j)),
            pl.BlockSpec((1, 1, tn), lambda l, j: (l, 0, j)),
        ],
        out_specs=pl.BlockSpec((1, MOD_ROWS, tn), lambda l, j: (l, 0, j)),
        out_shape=jax.ShapeDtypeStruct((depth, MOD_ROWS, n), F32),
        compiler_params=_cparams(("parallel", "parallel")),
        name="mod_vectors",
    )(c_rows, w_mod, b_mod.reshape(depth, 1, n))


def _rope_tables(seq, head_width):
    quarter = head_width // 4
    lane = np.arange(HEAD_DIM)
    within_head = lane % head_width
    use_col = (within_head // (2 * quarter)) == 1
    within_half = within_head % (2 * quarter)
    first = within_half < quarter
    freq = within_half % quarter
    t = jnp.arange(seq)
    row, col = t // GRID_W, t % GRID_W
    inv = ROPE_THETA ** (-jnp.arange(quarter, dtype=F32) / quarter)
    ang_row = row.astype(F32)[:, None] * inv[None, :]
    ang_col = col.astype(F32)[:, None] * inv[None, :]
    ang = jnp.where(use_col[None, :], ang_col[:, freq], ang_row[:, freq])
    cos, sin = jnp.cos(ang), jnp.sin(ang)
    zero = jnp.zeros_like(sin)
    return jnp.stack([cos, jnp.where(first[None, :], -sin, zero), jnp.where(first[None, :], zero, sin)])


def _rope(y, tab_ref, quarter):
    up = pltpu.roll(y, HEAD_DIM - quarter, 1)
    down = pltpu.roll(y, quarter, 1)
    return y * tab_ref[0] + up * tab_ref[1] + down * tab_ref[2]


def _col_plan(sizes):
    a_qk, _, a_v, b_q, b_kv, _, c_w, _, _ = sizes
    kinds = (["rope_a"] * (2 * a_qk // HEAD_DIM) + ["plain"] * (a_v // HEAD_DIM)
             + ["norm_q"] * (b_q // HEAD_DIM) + ["norm_k"] * (b_kv // HEAD_DIM)
             + ["plain"] * ((b_kv + 3 * c_w) // HEAD_DIM))
    return kinds


def _inproj_kernel(x_ref, mod_ref, g_ref, w_ref, gq_ref, gk_ref, ta_ref, tb_ref, o_ref, *,
                   kinds, chunk_units, use_rope):
    x = x_ref[...]
    m = mod_ref[...]
    ms = jnp.mean(x * x, axis=-1, keepdims=True)
    xn = x * lax.rsqrt(ms + EPS) * g_ref[...]
    xm = (xn * (1.0 + m[1:2]) + m[0:1]).astype(BF16)
    n_units = len(kinds)
    for u0 in range(0, n_units, chunk_units):
        u1 = min(u0 + chunk_units, n_units)
        y = jnp.dot(xm, w_ref[:, u0 * HEAD_DIM:u1 * HEAD_DIM], preferred_element_type=F32)
        for u in range(u0, u1):
            yu = y[:, (u - u0) * HEAD_DIM:(u - u0 + 1) * HEAD_DIM]
            kind = kinds[u]
            if kind in ("norm_q", "norm_k"):
                gain = gq_ref[...] if kind == "norm_q" else gk_ref[...]
                yu = yu * lax.rsqrt(jnp.mean(yu * yu, axis=-1, keepdims=True) + EPS) * gain
                if use_rope:
                    yu = _rope(yu, tb_ref, HEAD_DIM // 4)
            elif kind == "rope_a" and use_rope:
                yu = _rope(yu, ta_ref, HEAD_DIM // 8)
            o_ref[:, u * HEAD_DIM:(u + 1) * HEAD_DIM] = yu.astype(BF16)


def _in_projection(h2d, mods, mod_row_of_block, g_pre, w_bf16, g_qn, g_kn, tabs_a, tabs_b, *,
                   sizes, seq, tm, use_rope):
    rows, d = h2d.shape
    n = w_bf16.shape[1]
    kinds = _col_plan(sizes)
    blocks_per_seq = seq // tm
    kern = functools.partial(_inproj_kernel, kinds=kinds, chunk_units=4, use_rope=use_rope)
    return pl.pallas_call(
        kern,
        grid=(rows // tm,),
        in_specs=[
            pl.BlockSpec((tm, d), lambda i: (i, 0)),
            pl.BlockSpec((None, 6, d), lambda i: (mod_row_of_block(i), 0, 0)),
            pl.BlockSpec((1, d), lambda i: (0, 0)),
            pl.BlockSpec((d, n), lambda i: (0, 0), pipeline_mode=pl.Buffered(1)),
            pl.BlockSpec((1, HEAD_DIM), lambda i: (0, 0)),
            pl.BlockSpec((1, HEAD_DIM), lambda i: (0, 0)),
            pl.BlockSpec((3, tm, HEAD_DIM), lambda i: (0, i % blocks_per_seq, 0)),
            pl.BlockSpec((3, tm, HEAD_DIM), lambda i: (0, i % blocks_per_seq, 0)),
        ],
        out_specs=pl.BlockSpec((tm, n), lambda i: (i, 0)),
        out_shape=jax.ShapeDtypeStruct((rows, n), BF16),
        compiler_params=_cparams(("parallel",)),
        name="in_projection",
    )(h2d, mods, g_pre, w_bf16, g_qn, g_kn, tabs_a, tabs_b)


def _qk(q, k):
    return lax.dot_general(q, k, (((1,), (1,)), ((), ())), preferred_element_type=F32)


def _online_step(carry, s, v):
    m, l, acc = carry
    m_new = jnp.maximum(m, jnp.max(s, axis=-1, keepdims=True))
    alpha = jnp.exp(m - m_new)
    p = jnp.exp(s - m_new)
    l = alpha * l + jnp.sum(p, axis=-1, keepdims=True)
    acc = alpha * acc + jnp.dot(p.astype(BF16), v, preferred_element_type=F32)
    return m_new, l, acc


def _softmax_over_sources(score_fn, sources, tq, tk):
    carry = (jnp.full((tq, 1), NEG_BIG, F32), jnp.zeros((tq, 1), F32), jnp.zeros((tq, HEAD_DIM), F32))
    for k_ref, v_ref in sources:
        t = k_ref.shape[0]
        step = min(tk, t)
        n_chunks = t // step
        if n_chunks == 1:
            carry = _online_step(carry, score_fn(k_ref[...]), v_ref[...])
        else:
            def body(c, cr, k_ref=k_ref, v_ref=v_ref, step=step):
                rows = pl.ds(pl.multiple_of(c * step, step), step)
                return _online_step(cr, score_fn(k_ref[rows, :]), v_ref[rows, :])
            carry = lax.fori_loop(0, n_chunks, body, carry)
    _, l, acc = carry
    return acc / l


def _gqa_kernel(*refs, group, tk, scale, n_sources):
    q_ref = refs[0]
    kv = refs[1:1 + 2 * n_sources]
    o_ref = refs[1 + 2 * n_sources]
    sources = [(kv[2 * s], kv[2 * s + 1]) for s in range(n_sources)]
    tq = q_ref.shape[0]
    for g in range(group):
        q = q_ref[:, g * HEAD_DIM:(g + 1) * HEAD_DIM]
        out = _softmax_over_sources(lambda k, q=q: _qk(q, k) * scale, sources, tq, tk)
        o_ref[:, g * HEAD_DIM:(g + 1) * HEAD_DIM] = out.astype(BF16)


def _gqa_attention(q_arr, q_col, kv_arrs, k_col, v_col, *, batch, sq, n_kv, group, tq, tk, name):
    nq = sq // tq
    gw = group * HEAD_DIM
    in_specs = [pl.BlockSpec((tq, gw), lambda b, h, i: (b * nq + i, q_col // group + h))]
    args = [q_arr]
    for arr, t in kv_arrs:
        in_specs.append(pl.BlockSpec((t, HEAD_DIM), lambda b, h, i: (b, k_col + h)))
        in_specs.append(pl.BlockSpec((t, HEAD_DIM), lambda b, h, i: (b, v_col + h)))
        args += [arr, arr]
    kern = functools.partial(_gqa_kernel, group=group, tk=tk, scale=HEAD_DIM ** -0.5,
                             n_sources=len(kv_arrs))
    return pl.pallas_call(
        kern,
        grid=(batch, n_kv, nq),
        in_specs=in_specs,
        out_specs=pl.BlockSpec((tq, gw), lambda b, h, i: (b * nq + i, h)),
        out_shape=jax.ShapeDtypeStruct((batch * sq, n_kv * gw), BF16),
        compiler_params=_cparams(("parallel", "parallel", "parallel")),
        name=name,
    )(*args)


def _diff_kernel(*refs, tk, lam_init, n_sources):
    q_ref, lq1, lk1, lq2, lk2, g_ref = refs[:6]
    kv = refs[6:6 + 2 * n_sources]
    o_ref = refs[6 + 2 * n_sources]
    sources = [(kv[2 * s], kv[2 * s + 1]) for s in range(n_sources)]
    tq = q_ref.shape[0]
    half = HEAD_DIM // 2
    lam = (jnp.exp(jnp.sum(lq1[...] * lk1[...], axis=-1, keepdims=True))
           - jnp.exp(jnp.sum(lq2[...] * lk2[...], axis=-1, keepdims=True)) + lam_init)
    q = q_ref[...]
    lane = lax.broadcasted_iota(jnp.int32, q.shape, 1)
    scale = jnp.asarray(half ** -0.5, BF16)
    zero = jnp.zeros_like(q)
    q1 = jnp.where(lane < half, q, zero) * scale
    q2 = jnp.where(lane < half, zero, q) * scale
    o1 = _softmax_over_sources(lambda k: _qk(q1, k), sources, tq, tk)
    o2 = _softmax_over_sources(lambda k: _qk(q2, k), sources, tq, tk)
    o = o1 - lam * o2
    o = o * lax.rsqrt(jnp.mean(o * o, axis=-1, keepdims=True) + EPS) * g_ref[...]
    o_ref[...] = (o * (1.0 - lam_init)).astype(BF16)


def _diff_attention(q_arr, kv_arrs, lams, g_diff, *, batch, sq, heads, q_col, k_col, v_col,
                    tq, tk, lam_init, name):
    nq = sq // tq
    half = HEAD_DIM // 2
    small = lambda w: pl.BlockSpec((1, w), lambda b, h, i: (0, 0))
    in_specs = [pl.BlockSpec((tq, HEAD_DIM), lambda b, h, i: (b * nq + i, q_col + h)),
                small(half), small(half), small(half), small(half), small(HEAD_DIM)]
    args = [q_arr, *lams, g_diff]
    for arr, t in kv_arrs:
        in_specs.append(pl.BlockSpec((t, HEAD_DIM), lambda b, h, i: (b, k_col + h)))
        in_specs.append(pl.BlockSpec((t, HEAD_DIM), lambda b, h, i: (b, v_col + h)))
        args += [arr, arr]
    kern = functools.partial(_diff_kernel, tk=tk, lam_init=lam_init, n_sources=len(kv_arrs))
    return pl.pallas_call(
        kern,
        grid=(batch, heads, nq),
        in_specs=in_specs,
        out_specs=pl.BlockSpec((tq, HEAD_DIM), lambda b, h, i: (b * nq + i, h)),
        out_shape=jax.ShapeDtypeStruct((batch * sq, heads * HEAD_DIM), BF16),
        compiler_params=_cparams(("parallel", "parallel", "parallel")),
        name=name,
    )(*args)


def _na_plan(seq, tq, wr_max, win_rows):
    rows_n = seq // GRID_W
    wr = min(wr_max, rows_n)
    r_per = tq // GRID_W
    n_blocks = seq // tq
    wk = win_rows * GRID_W
    qi = np.arange(tq)
    kj = np.arange(wk)
    starts, idxs, valids = [], [], []
    for blk in range(n_blocks):
        rf = blk * r_per
        lo = int(np.clip(rf - wr // 2, 0, rows_n - wr))
        w0 = min(lo, rows_n - win_rows)
        r = rf + qi // GRID_W
        qc = qi % GRID_W
        kr = w0 + kj // GRID_W
        kc = kj % GRID_W
        r0 = np.clip(r - wr // 2, 0, rows_n - wr)
        row_ok = (kr[None, :] >= r0[:, None]) & (kr[None, :] < r0[:, None] + wr)
        c0 = np.clip(qc - NA_COLS // 2, 0, GRID_W - NA_COLS)
        col_ok = (kc[None, :] >= c0[:, None]) & (kc[None, :] < c0[:, None] + NA_COLS)
        roff = kr[None, :] - r[:, None] + wr_max - 1
        coff = np.clip(kc[None, :] - qc[:, None], -(NA_COLS - 1), NA_COLS - 1) + NA_COLS - 1
        valid = row_ok & col_ok
        idx = np.where(valid, roff * (2 * NA_COLS - 1) + coff, 0)
        starts.append(w0 * GRID_W)
        idxs.append(idx)
        valids.append(valid)
    keys = [i.tobytes() + v.tobytes() for i, v in zip(idxs, valids)]
    uniq = {}
    pids = []
    for b, key in enumerate(keys):
        if key not in uniq:
            uniq[key] = (len(uniq), b)
        pids.append(uniq[key][0])
    reps = [b for _, b in sorted(uniq.values())]
    idx_u = np.stack([idxs[b] for b in reps])
    valid_u = np.stack([valids[b] for b in reps])
    return (np.asarray(starts, np.int32), np.asarray(pids, np.int32), idx_u, valid_u)


def _na_kernel(w0_ref, pid_ref, q_ref, kc_ref, vc_ref, k_ref, v_ref, bias_ref, o_ref, *,
               heads, wk, scale):
    del pid_ref
    w0 = pl.multiple_of(w0_ref[pl.program_id(1)], GRID_W)
    win = pl.ds(w0, wk)
    for h in range(heads):
        lanes = slice(h * HEAD_DIM, (h + 1) * HEAD_DIM)
        q = q_ref[:, lanes]
        s_w = _qk(q, k_ref[win, lanes]) * scale + bias_ref[h]
        s_c = _qk(q, kc_ref[:, lanes]) * scale
        m = jnp.maximum(jnp.max(s_w, axis=-1, keepdims=True), jnp.max(s_c, axis=-1, keepdims=True))
        p_w = jnp.exp(s_w - m)
        p_c = jnp.exp(s_c - m)
        l = jnp.sum(p_w, axis=-1, keepdims=True) + jnp.sum(p_c, axis=-1, keepdims=True)
        acc = (jnp.dot(p_w.astype(BF16), v_ref[win, lanes], preferred_element_type=F32)
               + jnp.dot(p_c.astype(BF16), vc_ref[:, lanes], preferred_element_type=F32))
        o_ref[:, lanes] = (acc / l).astype(BF16)


def _na_attention(qkv, qkv_c, rpb_l, *, batch, seq, ctx_len, heads, q_col, k_col, v_col, tq):
    wr_max = (rpb_l.shape[1] + 1) // 2
    r_per = tq // GRID_W
    win_rows = min(-(-(r_per + wr_max - 1) // 4) * 4, seq // GRID_W)
    wk = win_rows * GRID_W
    starts, pids, idx_u, valid_u = _na_plan(seq, tq, wr_max, win_rows)
    flat = rpb_l.reshape(heads, -1).astype(F32)
    bias = jnp.where(valid_u[None], flat[:, idx_u], NEG_BIG)
    nq = seq // tq
    w = heads * HEAD_DIM
    kern = functools.partial(_na_kernel, heads=heads, wk=wk, scale=HEAD_DIM ** -0.5)
    grid_spec = pltpu.PrefetchScalarGridSpec(
        num_scalar_prefetch=2,
        grid=(batch, nq),
        in_specs=[
            pl.BlockSpec((tq, w), lambda b, i, w0, pid: (b * nq + i, q_col // heads)),
            pl.BlockSpec((ctx_len, w), lambda b, i, w0, pid: (b, k_col // heads)),
            pl.BlockSpec((ctx_len, w), lambda b, i, w0, pid: (b, v_col // heads)),
            pl.BlockSpec((seq, w), lambda b, i, w0, pid: (b, k_col // heads)),
            pl.BlockSpec((seq, w), lambda b, i, w0, pid: (b, v_col // heads)),
            pl.BlockSpec((heads, None, tq, wk), lambda b, i, w0, pid: (0, pid[i], 0, 0)),
        ],
        out_specs=pl.BlockSpec((tq, w), lambda b, i, w0, pid: (b * nq + i, 0)),
    )
    return pl.pallas_call(
        kern,
        grid_spec=grid_spec,
        out_shape=jax.ShapeDtypeStruct((batch * seq, w), BF16),
        compiler_params=_cparams(("parallel", "parallel")),
        name="na_attention",
    )(jnp.asarray(starts), jnp.asarray(pids), qkv, qkv_c, qkv_c, qkv, qkv, bias)


def _outproj_kernel(oa_ref, ob_ref, on_ref, w_ref, h_ref, mod_ref, g_ref, o_ref, mix_ref):
    wa, wb = oa_ref.shape[1], ob_ref.shape[1]
    mix_ref[:, :wa] = oa_ref[...]
    mix_ref[:, wa:wa + wb] = ob_ref[...]
    mix_ref[:, wa + wb:] = on_ref[...]
    y = jnp.dot(mix_ref[...], w_ref[...], preferred_element_type=F32)
    yn = y * lax.rsqrt(jnp.mean(y * y, axis=-1, keepdims=True) + EPS) * g_ref[...]
    o_ref[...] = h_ref[...] + mod_ref[2:3] * yn


def _out_projection(oa, ob, on, w_bf16, h2d, mods, mod_row_of_block, g_post, *, tm):
    rows, d = h2d.shape
    kdim = w_bf16.shape[0]
    return pl.pallas_call(
        _outproj_kernel,
        grid=(rows // tm,),
        in_specs=[
            pl.BlockSpec((tm, oa.shape[1]), lambda i: (i, 0)),
            pl.BlockSpec((tm, ob.shape[1]), lambda i: (i, 0)),
            pl.BlockSpec((tm, on.shape[1]), lambda i: (i, 0)),
            pl.BlockSpec((kdim, d), lambda i: (0, 0), pipeline_mode=pl.Buffered(1)),
            pl.BlockSpec((tm, d), lambda i: (i, 0)),
            pl.BlockSpec((None, 6, d), lambda i: (mod_row_of_block(i), 0, 0)),
            pl.BlockSpec((1, d), lambda i: (0, 0)),
        ],
        out_specs=pl.BlockSpec((tm, d), lambda i: (i, 0)),
        out_shape=jax.ShapeDtypeStruct((rows, d), F32),
        scratch_shapes=[pltpu.VMEM((tm, kdim), BF16)],
        compiler_params=_cparams(("parallel",)),
        name="out_projection",
    )(oa, ob, on, w_bf16, h2d, mods, g_post)


def _ffn_kernel(h_ref, mod_ref, gpre_ref, wg_ref, wu_ref, wd_ref, gpost_ref, o_ref, xm_ref, acc_ref):
    f = pl.program_id(1)

    @pl.when(f == 0)
    def _():
        x = h_ref[...]
        ms = jnp.mean(x * x, axis=-1, keepdims=True)
        xn = x * lax.rsqrt(ms + EPS) * gpre_ref[...]
        xm_ref[...] = (xn * (1.0 + mod_ref[4:5]) + mod_ref[3:4]).astype(BF16)
        acc_ref[...] = jnp.zeros_like(acc_ref)

    xm = xm_ref[...]
    gate = jnp.dot(xm, wg_ref[...], preferred_element_type=F32)
    up = jnp.dot(xm, wu_ref[...], preferred_element_type=F32)
    act = (gate * jax.nn.sigmoid(gate) * up).astype(BF16)
    acc_ref[...] += jnp.dot(act, wd_ref[...], preferred_element_type=F32)

    @pl.when(f == pl.num_programs(1) - 1)
    def _():
        y = acc_ref[...]
        yn = y * lax.rsqrt(jnp.mean(y * y, axis=-1, keepdims=True) + EPS) * gpost_ref[...]
        o_ref[...] = h_ref[...] + mod_ref[5:6] * yn


def _ffn(h2d, mods, mod_row_of_block, g_pre, wg, wu, wd, g_post, *, tm, tf):
    rows, d = h2d.shape
    ff = wg.shape[1]
    return pl.pallas_call(
        _ffn_kernel,
        grid=(rows // tm, ff // tf),
        in_specs=[
            pl.BlockSpec((tm, d), lambda i, f: (i, 0)),
            pl.BlockSpec((None, 6, d), lambda i, f: (mod_row_of_block(i), 0, 0)),
            pl.BlockSpec((1, d), lambda i, f: (0, 0)),
            pl.BlockSpec((d, tf), lambda i, f: (0, f)),
            pl.BlockSpec((d, tf), lambda i, f: (0, f)),
            pl.BlockSpec((tf, d), lambda i, f: (f, 0)),
            pl.BlockSpec((1, d), lambda i, f: (0, 0)),
        ],
        out_specs=pl.BlockSpec((tm, d), lambda i, f: (i, 0)),
        out_shape=jax.ShapeDtypeStruct((rows, d), F32),
        scratch_shapes=[pltpu.VMEM((tm, d), BF16), pltpu.VMEM((tm, d), F32)],
        compiler_params=_cparams(("parallel", "arbitrary")),
        name="ffn",
    )(h2d, mods, g_pre, wg, wu, wd, g_post)


def kernel(x, c, ctx, c_ctx, w_mod, b_mod, g_pre1, g_post1, g_pre2, g_post2, w_in, w_out,
           lam_q1, lam_k1, lam_q2, lam_k2, g_diff, g_qn, g_kn, rpb, w_gate, w_up, w_down):
    batch, seq, d = x.shape
    ctx_len = ctx.shape[1]
    depth = w_mod.shape[0]
    n_heads = d // HEAD_DIM
    a_heads, b_heads, c_heads = n_heads // 4, n_heads // 2, n_heads // 4
    b_kv = b_heads // 4
    group = b_heads // b_kv
    a_qk = a_v = a_heads * HEAD_DIM
    b_q, b_kvw, c_w = b_heads * HEAD_DIM, b_kv * HEAD_DIM, c_heads * HEAD_DIM
    sizes = (a_qk, a_qk, a_v, b_q, b_kvw, b_kvw, c_w, c_w, c_w)
    starts = np.cumsum((0,) + sizes)[:-1] // HEAD_DIM
    qa_c, ka_c, va_c, qb_c, kb_c, vb_c, qn_c, kn_c, vn_c = (int(s) for s in starts)
    assert batch + 1 <= MOD_ROWS and seq % GRID_W == 0

    tm = _pick(seq, (512, 256, 128))
    tm_c = _pick(ctx_len, (512, 256, 128))
    tq = _pick(seq, (256, 128))
    tq_c = _pick(ctx_len, (256, 128))
    tk = 512
    tf = _pick(w_gate.shape[2], (512, 256, 128))

    c_rows = jnp.zeros((MOD_ROWS, d), F32).at[:batch].set(c).at[batch].set(c_ctx)
    mods = _mod_vectors(c_rows, w_mod, b_mod).reshape(depth, MOD_ROWS, 6, d)
    tabs_a = _rope_tables(seq, HEAD_DIM // 2)
    tabs_b = _rope_tables(seq, HEAD_DIM)
    tabs_c = jnp.zeros((3, tm_c, HEAD_DIM), F32)

    lat_row = lambda i: i // (seq // tm)
    ctx_row = lambda i: batch
    row1 = lambda v: v.reshape(1, -1)

    h = x.reshape(batch * seq, d)
    hc = ctx.reshape(batch * ctx_len, d)
    for l in range(depth):
        need_ctx = l < depth - 1
        lam_init = 0.8 - 0.6 * math.exp(-0.3 * l)
        w_in_l = w_in[l].astype(BF16)
        w_out_l = w_out[l].astype(BF16)
        wg, wu, wd = w_gate[l].astype(BF16), w_up[l].astype(BF16), w_down[l].astype(BF16)
        lams = [row1(v[l]) for v in (lam_q1, lam_k1, lam_q2, lam_k2)]
        inproj = functools.partial(_in_projection, g_pre=row1(g_pre1[l]), w_bf16=w_in_l,
                                   g_qn=row1(g_qn[l]), g_kn=row1(g_kn[l]), sizes=sizes)
        qkv = inproj(h, mods[l], lat_row, tabs_a=tabs_a, tabs_b=tabs_b, seq=seq, tm=tm, use_rope=True)
        qkv_c = inproj(hc, mods[l], ctx_row, tabs_a=tabs_c, tabs_b=tabs_c, seq=tm_c, tm=tm_c,
                       use_rope=False)

        oa = _diff_attention(qkv, [(qkv_c, ctx_len), (qkv, seq)], lams, row1(g_diff[l]),
                             batch=batch, sq=seq, heads=a_heads, q_col=qa_c, k_col=ka_c, v_col=va_c,
                             tq=tq, tk=tk, lam_init=lam_init, name="diff_attention")
        ob = _gqa_attention(qkv, qb_c, [(qkv_c, ctx_len), (qkv, seq)], kb_c, vb_c, batch=batch,
                            sq=seq, n_kv=b_kv, group=group, tq=tq, tk=tk, name="gqa_attention")
        on = _na_attention(qkv, qkv_c, rpb[l], batch=batch, seq=seq, ctx_len=ctx_len, heads=c_heads,
                           q_col=qn_c, k_col=kn_c, v_col=vn_c, tq=tq)
        h_mid = _out_projection(oa, ob, on, w_out_l, h, mods[l], lat_row, row1(g_post1[l]), tm=tm)
        h_new = _ffn(h_mid, mods[l], lat_row, row1(g_pre2[l]), wg, wu, wd, row1(g_post2[l]), tm=tm, tf=tf)

        if need_ctx:
            oac = _diff_attention(qkv_c, [(qkv_c, ctx_len)], lams, row1(g_diff[l]), batch=batch,
                                  sq=ctx_len, heads=a_heads, q_col=qa_c, k_col=ka_c, v_col=va_c,
                                  tq=tq_c, tk=tk, lam_init=lam_init, name="diff_attention_ctx")
            obc = _gqa_attention(qkv_c, qb_c, [(qkv_c, ctx_len)], kb_c, vb_c, batch=batch, sq=ctx_len,
                                 n_kv=b_kv, group=group, tq=tq_c, tk=tk, name="gqa_attention_ctx")
            onc = _gqa_attention(qkv_c, qn_c, [(qkv_c, ctx_len)], kn_c, vn_c, batch=batch, sq=ctx_len,
                                 n_kv=c_heads, group=1, tq=tq_c, tk=tk, name="dense_attention_ctx")
            hc_mid = _out_projection(oac, obc, onc, w_out_l, hc, mods[l], ctx_row, row1(g_post1[l]),
                                     tm=tm_c)
            hc = _ffn(hc_mid, mods[l], ctx_row, row1(g_pre2[l]), wg, wu, wd, row1(g_post2[l]),
                      tm=tm_c, tf=tf)
        h = h_new
    return h.reshape(batch, seq, d)
```

```python
import functools
import math

import numpy as np
import jax
import jax.numpy as jnp
from jax import lax
from jax.experimental import pallas as pl
from jax.experimental.pallas import tpu as pltpu

GRID_W = 64
HEAD_DIM = 128
NA_COLS = 16
ROPE_THETA = 10000.0
EPS = 1e-6
NEG_BIG = -1e30
LOG2E = math.log2(math.e)
MOD_ROWS = 8
V7X_VMEM_LIMIT = 56 * 1024 * 1024

F32 = jnp.float32
BF16 = jnp.bfloat16


def _cparams(sem):
    return pltpu.CompilerParams(dimension_semantics=sem, vmem_limit_bytes=V7X_VMEM_LIMIT)


def _pick(n, candidates):
    for c in candidates:
        if n % c == 0:
            return c
    raise ValueError(f"no tile in {candidates} divides {n}")


def _mod_kernel(c_ref, w_ref, b_ref, o_ref):
    c = c_ref[...]
    a = (c * jax.nn.sigmoid(c)).astype(BF16)
    o_ref[0] = jnp.dot(a, w_ref[0].astype(BF16), preferred_element_type=F32) + b_ref[0]


def _mod_vectors(c_rows, w_mod, b_mod):
    depth, d, n = w_mod.shape
    tn = _pick(n, (1024, 512, 256, 128))
    return pl.pallas_call(
        _mod_kernel,
        grid=(depth, n // tn),
        in_specs=[
            pl.BlockSpec((MOD_ROWS, d), lambda l, j: (0, 0)),
            pl.BlockSpec((1, d, tn), lambda l, j: (l, 0, j)),
            pl.BlockSpec((1, 1, tn), lambda l, j: (l, 0, j)),
        ],
        out_specs=pl.BlockSpec((1, MOD_ROWS, tn), lambda l, j: (l, 0, j)),
        out_shape=jax.ShapeDtypeStruct((depth, MOD_ROWS, n), F32),
        compiler_params=_cparams(("parallel", "parallel")),
        name="mod_vectors",
    )(c_rows, w_mod, b_mod.reshape(depth, 1, n))


def _rope_tables(seq, head_width):
    quarter = head_width // 4
    lane = np.arange(HEAD_DIM)
    within_head = lane % head_width
    use_col = (within_head // (2 * quarter)) == 1
    within_half = within_head % (2 * quarter)
    first = within_half < quarter
    freq = within_half % quarter
    t = jnp.arange(seq)
    row, col = t // GRID_W, t % GRID_W
    inv = ROPE_THETA ** (-jnp.arange(quarter, dtype=F32) / quarter)
    ang_row = row.astype(F32)[:, None] * inv[None, :]
    ang_col = col.astype(F32)[:, None] * inv[None, :]
    ang = jnp.where(use_col[None, :], ang_col[:, freq], ang_row[:, freq])
    cos, sin = jnp.cos(ang), jnp.sin(ang)
    zero = jnp.zeros_like(sin)
    return jnp.stack([cos, jnp.where(first[None, :], -sin, zero), jnp.where(first[None, :], zero, sin)])


def _rope(y, tab_ref, quarter):
    up = pltpu.roll(y, HEAD_DIM - quarter, 1)
    down = pltpu.roll(y, quarter, 1)
    return y * tab_ref[0] + up * tab_ref[1] + down * tab_ref[2]


def _col_plan(sizes):
    a_qk, _, a_v, b_q, b_kv, _, c_w, _, _ = sizes
    u = lambda width: width // HEAD_DIM
    s_a = (HEAD_DIM // 2) ** -0.5 * LOG2E
    s_b = HEAD_DIM ** -0.5 * LOG2E
    plan = ([("a", None, s_a, None)] * u(a_qk) + [("a", None, None, None)] * u(a_qk)
            + [(None, None, None, i) for i in range(u(a_v))]
            + [("b", "q", s_b, None)] * u(b_q) + [("b", "k", None, None)] * u(b_kv)
            + [(None, None, None, u(a_v) + i) for i in range(u(b_kv))]
            + [(None, None, s_b, None)] * u(c_w) + [(None, None, None, None)] * (2 * u(c_w)))
    return plan


def _inproj_kernel(x_ref, mod_ref, g_ref, w_ref, gq_ref, gk_ref, ta_ref, tb_ref, o_ref, vt_ref, *,
                   plan, chunk_units, use_rope):
    x = x_ref[...]
    m = mod_ref[...]
    ms = jnp.mean(x * x, axis=-1, keepdims=True)
    xn = x * lax.rsqrt(ms + EPS) * g_ref[...]
    xm = (xn * (1.0 + m[1:2]) + m[0:1]).astype(BF16)
    n_units = len(plan)
    for u0 in range(0, n_units, chunk_units):
        u1 = min(u0 + chunk_units, n_units)
        y = jnp.dot(xm, w_ref[:, u0 * HEAD_DIM:u1 * HEAD_DIM], preferred_element_type=F32)
        for u in range(u0, u1):
            yu = y[:, (u - u0) * HEAD_DIM:(u - u0 + 1) * HEAD_DIM]
            rope, norm, q_scale, vt_slot = plan[u]
            if norm is not None:
                gain = gq_ref[...] if norm == "q" else gk_ref[...]
                yu = yu * lax.rsqrt(jnp.mean(yu * yu, axis=-1, keepdims=True) + EPS) * gain
            if rope is not None and use_rope:
                yu = _rope(yu, ta_ref, HEAD_DIM // 8) if rope == "a" else _rope(yu, tb_ref, HEAD_DIM // 4)
            if q_scale is not None:
                yu = yu * q_scale
            o_ref[:, u * HEAD_DIM:(u + 1) * HEAD_DIM] = yu.astype(BF16)
            if vt_slot is not None:
                vt_ref[vt_slot * HEAD_DIM:(vt_slot + 1) * HEAD_DIM, :] = yu.T.astype(BF16)


def _in_projection(h2d, mods, mod_row_of_block, g_pre, w_bf16, g_qn, g_kn, tabs_a, tabs_b, *,
                   sizes, batch, seq, tm, use_rope):
    rows, d = h2d.shape
    n = w_bf16.shape[1]
    plan = _col_plan(sizes)
    n_vt = sum(1 for p in plan if p[3] is not None)
    bps = seq // tm
    kern = functools.partial(_inproj_kernel, plan=plan, chunk_units=4, use_rope=use_rope)
    return pl.pallas_call(
        kern,
        grid=(rows // tm,),
        in_specs=[
            pl.BlockSpec((tm, d), lambda i: (i, 0)),
            pl.BlockSpec((None, 6, d), lambda i: (mod_row_of_block(i), 0, 0)),
            pl.BlockSpec((1, d), lambda i: (0, 0)),
            pl.BlockSpec((d, n), lambda i: (0, 0), pipeline_mode=pl.Buffered(1)),
            pl.BlockSpec((1, HEAD_DIM), lambda i: (0, 0)),
            pl.BlockSpec((1, HEAD_DIM), lambda i: (0, 0)),
            pl.BlockSpec((3, tm, HEAD_DIM), lambda i: (0, i % bps, 0)),
            pl.BlockSpec((3, tm, HEAD_DIM), lambda i: (0, i % bps, 0)),
        ],
        out_specs=[pl.BlockSpec((tm, n), lambda i: (i, 0)),
                   pl.BlockSpec((None, n_vt * HEAD_DIM, tm), lambda i: (i // bps, 0, i % bps))],
        out_shape=[jax.ShapeDtypeStruct((rows, n), BF16),
                   jax.ShapeDtypeStruct((batch, n_vt * HEAD_DIM, seq), BF16)],
        compiler_params=_cparams(("parallel",)),
        name="in_projection",
    )(h2d, mods, g_pre, w_bf16, g_qn, g_kn, tabs_a, tabs_b)


def _fold_rows(x, op):
    slabs = [x[r:r + 8] for r in range(0, x.shape[0], 8)]
    while len(slabs) > 1:
        nxt = [op(slabs[i], slabs[i + 1]) for i in range(0, len(slabs) - 1, 2)]
        if len(slabs) % 2:
            nxt.append(slabs[-1])
        slabs = nxt
    return slabs[0]


def _scores(q_list, k):
    return [lax.dot_general(k[:, u * HEAD_DIM:(u + 1) * HEAD_DIM], q, (((1,), (1,)), ((), ())),
                            preferred_element_type=F32) for q, u, _ in q_list]


def _softmax_pv(stats, scores, q_list, vt, acc_ref):
    probs = []
    for (m, l), s in zip(stats, scores):
        m_new = jnp.maximum(m, jnp.max(_fold_rows(s, jnp.maximum), axis=0, keepdims=True))
        alpha = jnp.exp2(m - m_new)
        p = jnp.exp2(s - m_new)
        l = alpha * l + jnp.sum(_fold_rows(p, jnp.add), axis=0, keepdims=True)
        probs.append((m_new, l, alpha, p.astype(BF16)))
    new = []
    for c, ((m_new, l, alpha, p), (_, _, v_unit)) in enumerate(zip(probs, q_list)):
        vv = vt[v_unit * HEAD_DIM:(v_unit + 1) * HEAD_DIM, :]
        acc_ref[c] = alpha * acc_ref[c] + jnp.dot(vv, p, preferred_element_type=F32)
        new.append((m_new, l))
    return tuple(new)


def _attn_chains(q_list, sources, tq, tk, s_refs, acc_ref):
    n_ch = len(q_list)
    acc_ref[...] = jnp.zeros_like(acc_ref)
    stats = tuple((jnp.full((1, tq), NEG_BIG, F32), jnp.zeros((1, tq), F32)) for _ in q_list)
    kc_ref, vtc_ref = sources[0]
    s_ctx = _scores(q_list, kc_ref[...])
    if len(sources) == 1:
        stats = _softmax_pv(stats, s_ctx, q_list, vtc_ref[...], acc_ref)
    else:
        k_ref, vt_ref = sources[1]
        n = k_ref.shape[0] // tk
        assert n >= 2 and n % 2 == 0 and n * tk == k_ref.shape[0]

        def keys_of(c):
            return pl.ds(c * tk if isinstance(c, int) else pl.multiple_of(c * tk, tk), tk)

        def put(slot, c):
            for i, s in enumerate(_scores(q_list, k_ref[keys_of(c), :])):
                s_refs[slot][i] = s

        def take(st, slot, c):
            return _softmax_pv(st, [s_refs[slot][i] for i in range(n_ch)], q_list,
                               vt_ref[:, keys_of(c)], acc_ref)

        put(0, 0)
        stats = _softmax_pv(stats, s_ctx, q_list, vtc_ref[...], acc_ref)

        def body(j, st):
            a = 2 * j
            put(1, a + 1)
            st = take(st, 0, a)
            put(0, a + 2)
            return take(st, 1, a + 1)

        stats = lax.fori_loop(0, n // 2 - 1, body, stats)
        put(1, n - 1)
        stats = take(stats, 0, n - 2)
        stats = take(stats, 1, n - 1)
    return [(acc_ref[c] / l).T for c, (_, l) in enumerate(stats)]


def _attn_scratch(n_chains, tq, tk):
    return [pltpu.VMEM((n_chains, tk, tq), F32), pltpu.VMEM((n_chains, tk, tq), F32),
            pltpu.VMEM((n_chains, HEAD_DIM, tq), F32)]


def _gqa_kernel(*refs, group, tk, n_sources):
    q_ref = refs[0]
    kv = refs[1:1 + 2 * n_sources]
    o_ref, s0_ref, s1_ref, acc_ref = refs[1 + 2 * n_sources:]
    sources = [(kv[2 * s], kv[2 * s + 1]) for s in range(n_sources)]
    tq = q_ref.shape[0]
    q_list = [(q_ref[:, g * HEAD_DIM:(g + 1) * HEAD_DIM], 0, 0) for g in range(group)]
    outs = _attn_chains(q_list, sources, tq, tk, (s0_ref, s1_ref), acc_ref)
    for g, o in enumerate(outs):
        o_ref[:, g * HEAD_DIM:(g + 1) * HEAD_DIM] = o.astype(BF16)


def _gqa_attention(q_arr, q_col, kv_arrs, k_col, vt_unit, *, batch, sq, n_kv, group, tq, tk, name):
    nq = sq // tq
    gw = group * HEAD_DIM
    assert q_col % group == 0
    in_specs = [pl.BlockSpec((tq, gw), lambda b, h, i: (b * nq + i, q_col // group + h))]
    args = [q_arr]
    for k_arr, vt_arr, t in kv_arrs:
        in_specs.append(pl.BlockSpec((t, HEAD_DIM), lambda b, h, i: (b, k_col + h)))
        in_specs.append(pl.BlockSpec((None, HEAD_DIM, t), lambda b, h, i: (b, vt_unit + h, 0)))
        args += [k_arr, vt_arr]
    kern = functools.partial(_gqa_kernel, group=group, tk=tk, n_sources=len(kv_arrs))
    return pl.pallas_call(
        kern,
        grid=(batch, n_kv, nq),
        in_specs=in_specs,
        out_specs=pl.BlockSpec((tq, gw), lambda b, h, i: (b * nq + i, h)),
        out_shape=jax.ShapeDtypeStruct((batch * sq, n_kv * gw), BF16),
        scratch_shapes=_attn_scratch(group, tq, tk),
        compiler_params=_cparams(("parallel", "parallel", "parallel")),
        name=name,
    )(*args)


def _diff_kernel(*refs, hps, tk, lam_init, n_sources):
    q_ref, lq1, lk1, lq2, lk2, g_ref = refs[:6]
    kv = refs[6:6 + 2 * n_sources]
    o_ref, s0_ref, s1_ref, acc_ref = refs[6 + 2 * n_sources:]
    sources = [(kv[2 * s], kv[2 * s + 1]) for s in range(n_sources)]
    tq = q_ref.shape[0]
    half = HEAD_DIM // 2
    lam = (jnp.exp(jnp.sum(lq1[...] * lk1[...], axis=-1, keepdims=True))
           - jnp.exp(jnp.sum(lq2[...] * lk2[...], axis=-1, keepdims=True)) + lam_init)
    lane = lax.broadcasted_iota(jnp.int32, (tq, HEAD_DIM), 1)
    q_list = []
    for h in range(hps):
        q = q_ref[:, h * HEAD_DIM:(h + 1) * HEAD_DIM]
        zero = jnp.zeros_like(q)
        q_list.append((jnp.where(lane < half, q, zero), h, h))
        q_list.append((jnp.where(lane < half, zero, q), h, h))
    outs = _attn_chains(q_list, sources, tq, tk, (s0_ref, s1_ref), acc_ref)
    for h in range(hps):
        o = outs[2 * h] - lam * outs[2 * h + 1]
        o = o * lax.rsqrt(jnp.mean(o * o, axis=-1, keepdims=True) + EPS) * g_ref[...]
        o_ref[:, h * HEAD_DIM:(h + 1) * HEAD_DIM] = (o * (1.0 - lam_init)).astype(BF16)


def _diff_attention(q_arr, kv_arrs, lams, g_diff, *, batch, sq, heads, q_col, k_col, vt_unit,
                    hps, tq, tk, lam_init, name):
    nq = sq // tq
    half = HEAD_DIM // 2
    w = hps * HEAD_DIM
    assert heads % hps == 0 and q_col % hps == 0 and k_col % hps == 0 and vt_unit % hps == 0
    small = lambda width: pl.BlockSpec((1, width), lambda b, h, i: (0, 0))
    in_specs = [pl.BlockSpec((tq, w), lambda b, h, i: (b * nq + i, q_col // hps + h)),
                small(half), small(half), small(half), small(half), small(HEAD_DIM)]
    args = [q_arr, *lams, g_diff]
    for k_arr, vt_arr, t in kv_arrs:
        in_specs.append(pl.BlockSpec((t, w), lambda b, h, i: (b, k_col // hps + h)))
        in_specs.append(pl.BlockSpec((None, w, t), lambda b, h, i: (b, vt_unit // hps + h, 0)))
        args += [k_arr, vt_arr]
    kern = functools.partial(_diff_kernel, hps=hps, tk=tk, lam_init=lam_init, n_sources=len(kv_arrs))
    return pl.pallas_call(
        kern,
        grid=(batch, heads // hps, nq),
        in_specs=in_specs,
        out_specs=pl.BlockSpec((tq, w), lambda b, h, i: (b * nq + i, h)),
        out_shape=jax.ShapeDtypeStruct((batch * sq, heads * HEAD_DIM), BF16),
        scratch_shapes=_attn_scratch(2 * hps, tq, tk),
        compiler_params=_cparams(("parallel", "parallel", "parallel")),
        name=name,
    )(*args)


def _na_plan(seq, tq, wr_max, win_rows):
    rows_n = seq // GRID_W
    wr = min(wr_max, rows_n)
    r_per = tq // GRID_W
    outside = 2 * wr_max - 1
    starts, pats = [], []
    for blk in range(seq // tq):
        rf = blk * r_per
        lo = int(np.clip(rf - wr // 2, 0, rows_n - wr))
        w0 = min(lo, rows_n - win_rows)
        r = rf + np.arange(r_per)
        kr = w0 + np.arange(win_rows)
        r0 = np.clip(r - wr // 2, 0, rows_n - wr)
        row_ok = (kr[None, :] >= r0[:, None]) & (kr[None, :] < r0[:, None] + wr)
        roff = kr[None, :] - r[:, None] + wr_max - 1
        starts.append(w0 * GRID_W)
        pats.append(np.where(row_ok, roff, outside))
    uniq, pids = [], []
    for p in pats:
        for j, u in enumerate(uniq):
            if np.array_equal(u, p):
                pids.append(j)
                break
        else:
            pids.append(len(uniq))
            uniq.append(p)
    return np.asarray(starts, np.int32), np.asarray(pids, np.int32), np.stack(uniq)


def _na_bias(rpb_l, row_pats, tq, wk):
    heads = rpb_l.shape[0]
    cidx = np.arange(GRID_W)
    c0 = np.clip(cidx - NA_COLS // 2, 0, GRID_W - NA_COLS)
    col_ok = (cidx[None, :] >= c0[:, None]) & (cidx[None, :] < c0[:, None] + NA_COLS)
    coff = np.clip(cidx[None, :] - cidx[:, None], -(NA_COLS - 1), NA_COLS - 1) + NA_COLS - 1
    by_col = jnp.where(col_ok[None, None], rpb_l.astype(F32)[:, :, coff] * LOG2E, NEG_BIG)
    by_col = jnp.concatenate([by_col, jnp.full((heads, 1, GRID_W, GRID_W), NEG_BIG, F32)], axis=1)
    tiles = jnp.take(by_col, jnp.asarray(row_pats), axis=1)
    n_pat = row_pats.shape[0]
    return tiles.transpose(0, 1, 2, 4, 3, 5).reshape(heads, n_pat, tq, wk)


def _na_kernel(w0_ref, pid_ref, q_ref, kc_ref, vc_ref, k_ref, v_ref, bias_ref, o_ref, *, heads, wk):
    del pid_ref
    w0 = pl.multiple_of(w0_ref[pl.program_id(1)], GRID_W)
    win = pl.ds(w0, wk)
    for h in range(heads):
        lanes = slice(h * HEAD_DIM, (h + 1) * HEAD_DIM)
        q = q_ref[:, lanes]
        s_w = lax.dot_general(q, k_ref[win, lanes], (((1,), (1,)), ((), ())),
                              preferred_element_type=F32) + bias_ref[h]
        s_c = lax.dot_general(q, kc_ref[:, lanes], (((1,), (1,)), ((), ())), preferred_element_type=F32)
        m = jnp.maximum(jnp.max(s_w, axis=-1, keepdims=True), jnp.max(s_c, axis=-1, keepdims=True))
        p_w = jnp.exp2(s_w - m)
        p_c = jnp.exp2(s_c - m)
        l = jnp.sum(p_w, axis=-1, keepdims=True) + jnp.sum(p_c, axis=-1, keepdims=True)
        acc = (jnp.dot(p_w.astype(BF16), v_ref[win, lanes], preferred_element_type=F32)
               + jnp.dot(p_c.astype(BF16), vc_ref[:, lanes], preferred_element_type=F32))
        o_ref[:, lanes] = (acc / l).astype(BF16)


def _na_attention(qkv, qkv_c, rpb_l, *, batch, seq, ctx_len, heads, q_col, k_col, v_col, tq):
    wr_max = (rpb_l.shape[1] + 1) // 2
    r_per = tq // GRID_W
    win_rows = min(-(-(r_per + wr_max - 1) // 4) * 4, seq // GRID_W)
    wk = win_rows * GRID_W
    starts, pids, row_pats = _na_plan(seq, tq, wr_max, win_rows)
    bias = _na_bias(rpb_l, row_pats, tq, wk)
    nq = seq // tq
    w = heads * HEAD_DIM
    kern = functools.partial(_na_kernel, heads=heads, wk=wk)
    grid_spec = pltpu.PrefetchScalarGridSpec(
        num_scalar_prefetch=2,
        grid=(batch, nq),
        in_specs=[
            pl.BlockSpec((tq, w), lambda b, i, w0, pid: (b * nq + i, q_col // heads)),
            pl.BlockSpec((ctx_len, w), lambda b, i, w0, pid: (b, k_col // heads)),
            pl.BlockSpec((ctx_len, w), lambda b, i, w0, pid: (b, v_col // heads)),
            pl.BlockSpec((seq, w), lambda b, i, w0, pid: (b, k_col // heads)),
            pl.BlockSpec((seq, w), lambda b, i, w0, pid: (b, v_col // heads)),
            pl.BlockSpec((heads, None, tq, wk), lambda b, i, w0, pid: (0, pid[i], 0, 0)),
        ],
        out_specs=pl.BlockSpec((tq, w), lambda b, i, w0, pid: (b * nq + i, 0)),
    )
    return pl.pallas_call(
        kern,
        grid_spec=grid_spec,
        out_shape=jax.ShapeDtypeStruct((batch * seq, w), BF16),
        compiler_params=_cparams(("parallel", "parallel")),
        name="na_attention",
    )(jnp.asarray(starts), jnp.asarray(pids), qkv, qkv_c, qkv_c, qkv, qkv, bias)


def _dense_ctx_kernel(q_ref, k_ref, v_ref, o_ref, *, heads):
    for h in range(heads):
        lanes = slice(h * HEAD_DIM, (h + 1) * HEAD_DIM)
        s = lax.dot_general(q_ref[:, lanes], k_ref[:, lanes], (((1,), (1,)), ((), ())),
                            preferred_element_type=F32)
        p = jnp.exp2(s - jnp.max(s, axis=-1, keepdims=True))
        acc = jnp.dot(p.astype(BF16), v_ref[:, lanes], preferred_element_type=F32)
        o_ref[:, lanes] = (acc / jnp.sum(p, axis=-1, keepdims=True)).astype(BF16)


def _dense_ctx_attention(qkv_c, *, batch, ctx_len, heads, q_col, k_col, v_col):
    w = heads * HEAD_DIM
    spec = lambda col: pl.BlockSpec((ctx_len, w), lambda b: (b, col // heads))
    return pl.pallas_call(
        functools.partial(_dense_ctx_kernel, heads=heads),
        grid=(batch,),
        in_specs=[spec(q_col), spec(k_col), spec(v_col)],
        out_specs=pl.BlockSpec((ctx_len, w), lambda b: (b, 0)),
        out_shape=jax.ShapeDtypeStruct((batch * ctx_len, w), BF16),
        compiler_params=_cparams(("parallel",)),
        name="dense_attention_ctx",
    )(qkv_c, qkv_c, qkv_c)


def _outproj_kernel(oa_ref, ob_ref, on_ref, w_ref, h_ref, mod_ref, g_ref, o_ref, mix_ref):
    wa, wb = oa_ref.shape[1], ob_ref.shape[1]
    mix_ref[:, :wa] = oa_ref[...]
    mix_ref[:, wa:wa + wb] = ob_ref[...]
    mix_ref[:, wa + wb:] = on_ref[...]
    y = jnp.dot(mix_ref[...], w_ref[...], preferred_element_type=F32)
    yn = y * lax.rsqrt(jnp.mean(y * y, axis=-1, keepdims=True) + EPS) * g_ref[...]
    o_ref[...] = h_ref[...] + mod_ref[2:3] * yn


def _out_projection(oa, ob, on, w_bf16, h2d, mods, mod_row_of_block, g_post, *, tm):
    rows, d = h2d.shape
    kdim = w_bf16.shape[0]
    return pl.pallas_call(
        _outproj_kernel,
        grid=(rows // tm,),
        in_specs=[
            pl.BlockSpec((tm, oa.shape[1]), lambda i: (i, 0)),
            pl.BlockSpec((tm, ob.shape[1]), lambda i: (i, 0)),
            pl.BlockSpec((tm, on.shape[1]), lambda i: (i, 0)),
            pl.BlockSpec((kdim, d), lambda i: (0, 0), pipeline_mode=pl.Buffered(1)),
            pl.BlockSpec((tm, d), lambda i: (i, 0)),
            pl.BlockSpec((None, 6, d), lambda i: (mod_row_of_block(i), 0, 0)),
            pl.BlockSpec((1, d), lambda i: (0, 0)),
        ],
        out_specs=pl.BlockSpec((tm, d), lambda i: (i, 0)),
        out_shape=jax.ShapeDtypeStruct((rows, d), F32),
        scratch_shapes=[pltpu.VMEM((tm, kdim), BF16)],
        compiler_params=_cparams(("parallel",)),
        name="out_projection",
    )(oa, ob, on, w_bf16, h2d, mods, g_post)


def _ffn_kernel(h_ref, mod_ref, gpre_ref, wg_ref, wu_ref, wd_ref, gpost_ref, o_ref, xm_ref, acc_ref):
    f = pl.program_id(1)

    @pl.when(f == 0)
    def _():
        x = h_ref[...]
        ms = jnp.mean(x * x, axis=-1, keepdims=True)
        xn = x * lax.rsqrt(ms + EPS) * gpre_ref[...]
        xm_ref[...] = (xn * (1.0 + mod_ref[4:5]) + mod_ref[3:4]).astype(BF16)
        acc_ref[...] = jnp.zeros_like(acc_ref)

    xm = xm_ref[...]
    gate = jnp.dot(xm, wg_ref[...], preferred_element_type=F32)
    up = jnp.dot(xm, wu_ref[...], preferred_element_type=F32)
    act = (gate * jax.nn.sigmoid(gate) * up).astype(BF16)
    acc_ref[...] += jnp.dot(act, wd_ref[...], preferred_element_type=F32)

    @pl.when(f == pl.num_programs(1) - 1)
    def _():
        y = acc_ref[...]
        yn = y * lax.rsqrt(jnp.mean(y * y, axis=-1, keepdims=True) + EPS) * gpost_ref[...]
        o_ref[...] = h_ref[...] + mod_ref[5:6] * yn


def _ffn(h2d, mods, mod_row_of_block, g_pre, wg, wu, wd, g_post, *, tm, tf):
    rows, d = h2d.shape
    ff = wg.shape[1]
    return pl.pallas_call(
        _ffn_kernel,
        grid=(rows // tm, ff // tf),
        in_specs=[
            pl.BlockSpec((tm, d), lambda i, f: (i, 0)),
            pl.BlockSpec((None, 6, d), lambda i, f: (mod_row_of_block(i), 0, 0)),
            pl.BlockSpec((1, d), lambda i, f: (0, 0)),
            pl.BlockSpec((d, tf), lambda i, f: (0, f)),
            pl.BlockSpec((d, tf), lambda i, f: (0, f)),
            pl.BlockSpec((tf, d), lambda i, f: (f, 0)),
            pl.BlockSpec((1, d), lambda i, f: (0, 0)),
        ],
        out_specs=pl.BlockSpec((tm, d), lambda i, f: (i, 0)),
        out_shape=jax.ShapeDtypeStruct((rows, d), F32),
        scratch_shapes=[pltpu.VMEM((tm, d), BF16), pltpu.VMEM((tm, d), F32)],
        compiler_params=_cparams(("parallel", "arbitrary")),
        name="ffn",
    )(h2d, mods, g_pre, wg, wu, wd, g_post)


def kernel(x, c, ctx, c_ctx, w_mod, b_mod, g_pre1, g_post1, g_pre2, g_post2, w_in, w_out,
           lam_q1, lam_k1, lam_q2, lam_k2, g_diff, g_qn, g_kn, rpb, w_gate, w_up, w_down):
    batch, seq, d = x.shape
    ctx_len = ctx.shape[1]
    depth = w_mod.shape[0]
    n_heads = d // HEAD_DIM
    a_heads, b_heads, c_heads = n_heads // 4, n_heads // 2, n_heads // 4
    b_kv = b_heads // 4
    group = b_heads // b_kv
    a_qk = a_v = a_heads * HEAD_DIM
    b_q, b_kvw, c_w = b_heads * HEAD_DIM, b_kv * HEAD_DIM, c_heads * HEAD_DIM
    sizes = (a_qk, a_qk, a_v, b_q, b_kvw, b_kvw, c_w, c_w, c_w)
    starts = np.cumsum((0,) + sizes)[:-1] // HEAD_DIM
    qa_c, ka_c, _, qb_c, kb_c, _, qn_c, kn_c, vn_c = (int(s) for s in starts)
    vta_u, vtb_u = 0, a_heads
    assert batch + 1 <= MOD_ROWS and seq % GRID_W == 0

    tm = _pick(seq, (512, 256, 128))
    tm_c = _pick(ctx_len, (512, 256, 128))
    tq = _pick(seq, (256, 128))
    tq_c = _pick(ctx_len, (256, 128))
    tk = 512
    tf = _pick(w_gate.shape[2], (512, 256, 128))
    hps = 2 if a_heads % 2 == 0 else 1

    c_rows = jnp.zeros((MOD_ROWS, d), F32).at[:batch].set(c).at[batch].set(c_ctx)
    mods = _mod_vectors(c_rows, w_mod, b_mod).reshape(depth, MOD_ROWS, 6, d)
    tabs_a = _rope_tables(seq, HEAD_DIM // 2)
    tabs_b = _rope_tables(seq, HEAD_DIM)
    tabs_c = jnp.zeros((3, tm_c, HEAD_DIM), F32)

    lat_row = lambda i: i // (seq // tm)
    ctx_row = lambda i: batch
    row1 = lambda v: v.reshape(1, -1)

    h = x.reshape(batch * seq, d)
    hc = ctx.reshape(batch * ctx_len, d)
    for l in range(depth):
        need_ctx = l < depth - 1
        lam_init = 0.8 - 0.6 * math.exp(-0.3 * l)
        w_in_l = w_in[l].astype(BF16)
        w_out_l = w_out[l].astype(BF16)
        wg, wu, wd = w_gate[l].astype(BF16), w_up[l].astype(BF16), w_down[l].astype(BF16)
        lams = [row1(v[l]) for v in (lam_q1, lam_k1, lam_q2, lam_k2)]
        inproj = functools.partial(_in_projection, g_pre=row1(g_pre1[l]), w_bf16=w_in_l,
                                   g_qn=row1(g_qn[l]), g_kn=row1(g_kn[l]), sizes=sizes, batch=batch)
        qkv, vt = inproj(h, mods[l], lat_row, tabs_a=tabs_a, tabs_b=tabs_b, seq=seq, tm=tm, use_rope=True)
        qkv_c, vt_c = inproj(hc, mods[l], ctx_row, tabs_a=tabs_c, tabs_b=tabs_c, seq=ctx_len, tm=tm_c,
                             use_rope=False)
        both = [(qkv_c, vt_c, ctx_len), (qkv, vt, seq)]
        only_ctx = [(qkv_c, vt_c, ctx_len)]

        oa = _diff_attention(qkv, both, lams, row1(g_diff[l]), batch=batch, sq=seq, heads=a_heads,
                             q_col=qa_c, k_col=ka_c, vt_unit=vta_u, hps=hps, tq=tq, tk=tk,
                             lam_init=lam_init, name="diff_attention")
        ob = _gqa_attention(qkv, qb_c, both, kb_c, vtb_u, batch=batch, sq=seq, n_kv=b_kv, group=group,
                            tq=tq, tk=tk, name="gqa_attention")
        on = _na_attention(qkv, qkv_c, rpb[l], batch=batch, seq=seq, ctx_len=ctx_len, heads=c_heads,
                           q_col=qn_c, k_col=kn_c, v_col=vn_c, tq=tq)
        h_mid = _out_projection(oa, ob, on, w_out_l, h, mods[l], lat_row, row1(g_post1[l]), tm=tm)
        h_new = _ffn(h_mid, mods[l], lat_row, row1(g_pre2[l]), wg, wu, wd, row1(g_post2[l]), tm=tm, tf=tf)

        if need_ctx:
            oac = _diff_attention(qkv_c, only_ctx, lams, row1(g_diff[l]), batch=batch, sq=ctx_len,
                                  heads=a_heads, q_col=qa_c, k_col=ka_c, vt_unit=vta_u, hps=hps,
                                  tq=tq_c, tk=tk, lam_init=lam_init, name="diff_attention_ctx")
            obc = _gqa_attention(qkv_c, qb_c, only_ctx, kb_c, vtb_u, batch=batch, sq=ctx_len, n_kv=b_kv,
                                 group=group, tq=tq_c, tk=tk, name="gqa_attention_ctx")
            onc = _dense_ctx_attention(qkv_c, batch=batch, ctx_len=ctx_len, heads=c_heads, q_col=qn_c,
                                       k_col=kn_c, v_col=vn_c)
            hc_mid = _out_projection(oac, obc, onc, w_out_l, hc, mods[l], ctx_row, row1(g_post1[l]),
                                     tm=tm_c)
            hc = _ffn(hc_mid, mods[l], ctx_row, row1(g_pre2[l]), wg, wu, wd, row1(g_post2[l]),
                      tm=tm_c, tf=tf)
        h = h_new
    return h.reshape(batch, seq, d)
```

```python
import functools
import math

import numpy as np
import jax
import jax.numpy as jnp
from jax import lax
from jax.experimental import pallas as pl
from jax.experimental.pallas import tpu as pltpu

GRID_W = 64
HEAD_DIM = 128
NA_COLS = 16
ROPE_THETA = 10000.0
EPS = 1e-6
NEG_BIG = -1e30
LOG2E = math.log2(math.e)
MOD_ROWS = 8
V7X_VMEM_LIMIT = 56 * 1024 * 1024

F32 = jnp.float32
BF16 = jnp.bfloat16


def _cparams(sem):
    return pltpu.CompilerParams(dimension_semantics=sem, vmem_limit_bytes=V7X_VMEM_LIMIT)


def _pick(n, candidates):
    for c in candidates:
        if n % c == 0:
            return c
    raise ValueError(f"no tile in {candidates} divides {n}")


def _mod_kernel(c_ref, w_ref, b_ref, o_ref):
    c = c_ref[...]
    a = (c * jax.nn.sigmoid(c)).astype(BF16)
    o_ref[0] = jnp.dot(a, w_ref[0].astype(BF16), preferred_element_type=F32) + b_ref[0]


def _mod_vectors(c_rows, w_mod, b_mod):
    depth, d, n = w_mod.shape
    tn = _pick(n, (1024, 512, 256, 128))
    return pl.pallas_call(
        _mod_kernel,
        grid=(depth, n // tn),
        in_specs=[
            pl.BlockSpec((MOD_ROWS, d), lambda l, j: (0, 0)),
            pl.BlockSpec((1, d, tn), lambda l, j: (l, 0, j)),
            pl.BlockSpec((1, 1, tn), lambda l, j: (l, 0, j)),
        ],
        out_specs=pl.BlockSpec((1, MOD_ROWS, tn), lambda l, j: (l, 0, j)),
        out_shape=jax.ShapeDtypeStruct((depth, MOD_ROWS, n), F32),
        compiler_params=_cparams(("parallel", "parallel")),
        name="mod_vectors",
    )(c_rows, w_mod, b_mod.reshape(depth, 1, n))


def _cast_kernel(w_ref, o_ref):
    o_ref[...] = w_ref[...].astype(BF16)


def _to_bf16(w):
    depth, rows, cols = w.shape
    tr = _pick(rows, (256, 128))
    return pl.pallas_call(
        _cast_kernel,
        grid=(depth, rows // tr),
        in_specs=[pl.BlockSpec((None, tr, cols), lambda l, i: (l, i, 0))],
        out_specs=pl.BlockSpec((None, tr, cols), lambda l, i: (l, i, 0)),
        out_shape=jax.ShapeDtypeStruct(w.shape, BF16),
        compiler_params=_cparams(("parallel", "parallel")),
        name="weights_to_bf16",
    )(w)


def _rope_tables(seq, head_width):
    quarter = head_width // 4
    lane = np.arange(HEAD_DIM)
    within_head = lane % head_width
    use_col = (within_head // (2 * quarter)) == 1
    within_half = within_head % (2 * quarter)
    first = within_half < quarter
    freq = within_half % quarter
    t = jnp.arange(seq)
    row, col = t // GRID_W, t % GRID_W
    inv = ROPE_THETA ** (-jnp.arange(quarter, dtype=F32) / quarter)
    ang_row = row.astype(F32)[:, None] * inv[None, :]
    ang_col = col.astype(F32)[:, None] * inv[None, :]
    ang = jnp.where(use_col[None, :], ang_col[:, freq], ang_row[:, freq])
    cos, sin = jnp.cos(ang), jnp.sin(ang)
    zero = jnp.zeros_like(sin)
    return jnp.stack([cos, jnp.where(first[None, :], -sin, zero), jnp.where(first[None, :], zero, sin)])


def _rope(y, tab_ref, quarter):
    up = pltpu.roll(y, HEAD_DIM - quarter, 1)
    down = pltpu.roll(y, quarter, 1)
    return y * tab_ref[0] + up * tab_ref[1] + down * tab_ref[2]


def _col_plan(sizes):
    a_qk, _, a_v, b_q, b_kv, _, c_w, _, _ = sizes
    u = lambda width: width // HEAD_DIM
    s_a = (HEAD_DIM // 2) ** -0.5 * LOG2E
    s_b = HEAD_DIM ** -0.5 * LOG2E
    plan = ([("a", None, s_a, None)] * u(a_qk) + [("a", None, None, None)] * u(a_qk)
            + [(None, None, None, i) for i in range(u(a_v))]
            + [("b", "q", s_b, None)] * u(b_q) + [("b", "k", None, None)] * u(b_kv)
            + [(None, None, None, u(a_v) + i) for i in range(u(b_kv))]
            + [(None, None, s_b, None)] * u(c_w) + [(None, None, None, None)] * (2 * u(c_w)))
    return plan


def _inproj_kernel(x_ref, mod_ref, g_ref, w_ref, gq_ref, gk_ref, ta_ref, tb_ref, o_ref, vt_ref, *,
                   plan, chunk_units, use_rope):
    x = x_ref[...]
    m = mod_ref[...]
    ms = jnp.mean(x * x, axis=-1, keepdims=True)
    xn = x * lax.rsqrt(ms + EPS) * g_ref[...]
    xm = (xn * (1.0 + m[1:2]) + m[0:1]).astype(BF16)
    n_units = len(plan)
    for u0 in range(0, n_units, chunk_units):
        u1 = min(u0 + chunk_units, n_units)
        y = jnp.dot(xm, w_ref[:, u0 * HEAD_DIM:u1 * HEAD_DIM], preferred_element_type=F32)
        for u in range(u0, u1):
            yu = y[:, (u - u0) * HEAD_DIM:(u - u0 + 1) * HEAD_DIM]
            rope, norm, q_scale, vt_slot = plan[u]
            if norm is not None:
                gain = gq_ref[...] if norm == "q" else gk_ref[...]
                yu = yu * lax.rsqrt(jnp.mean(yu * yu, axis=-1, keepdims=True) + EPS) * gain
            if rope is not None and use_rope:
                yu = _rope(yu, ta_ref, HEAD_DIM // 8) if rope == "a" else _rope(yu, tb_ref, HEAD_DIM // 4)
            if q_scale is not None:
                yu = yu * q_scale
            o_ref[:, u * HEAD_DIM:(u + 1) * HEAD_DIM] = yu.astype(BF16)
            if vt_slot is not None:
                vt_ref[vt_slot * HEAD_DIM:(vt_slot + 1) * HEAD_DIM, :] = yu.T.astype(BF16)


def _in_projection(h2d, mods, mod_row_of_block, g_pre, w_bf16, g_qn, g_kn, tabs_a, tabs_b, *,
                   layer, sizes, batch, seq, tm, use_rope):
    rows, d = h2d.shape
    n = w_bf16.shape[2]
    plan = _col_plan(sizes)
    n_vt = sum(1 for p in plan if p[3] is not None)
    bps = seq // tm
    kern = functools.partial(_inproj_kernel, plan=plan, chunk_units=4, use_rope=use_rope)
    return pl.pallas_call(
        kern,
        grid=(rows // tm,),
        in_specs=[
            pl.BlockSpec((tm, d), lambda i: (i, 0)),
            pl.BlockSpec((None, 6, d), lambda i: (mod_row_of_block(i), 0, 0)),
            pl.BlockSpec((1, d), lambda i: (0, 0)),
            pl.BlockSpec((None, d, n), lambda i: (layer, 0, 0), pipeline_mode=pl.Buffered(1)),
            pl.BlockSpec((1, HEAD_DIM), lambda i: (0, 0)),
            pl.BlockSpec((1, HEAD_DIM), lambda i: (0, 0)),
            pl.BlockSpec((3, tm, HEAD_DIM), lambda i: (0, i % bps, 0)),
            pl.BlockSpec((3, tm, HEAD_DIM), lambda i: (0, i % bps, 0)),
        ],
        out_specs=[pl.BlockSpec((tm, n), lambda i: (i, 0)),
                   pl.BlockSpec((None, n_vt * HEAD_DIM, tm), lambda i: (i // bps, 0, i % bps))],
        out_shape=[jax.ShapeDtypeStruct((rows, n), BF16),
                   jax.ShapeDtypeStruct((batch, n_vt * HEAD_DIM, seq), BF16)],
        compiler_params=_cparams(("parallel",)),
        name="in_projection",
    )(h2d, mods, g_pre, w_bf16, g_qn, g_kn, tabs_a, tabs_b)


def _fold_rows(x, op):
    slabs = [x[r:r + 8] for r in range(0, x.shape[0], 8)]
    while len(slabs) > 1:
        nxt = [op(slabs[i], slabs[i + 1]) for i in range(0, len(slabs) - 1, 2)]
        if len(slabs) % 2:
            nxt.append(slabs[-1])
        slabs = nxt
    return slabs[0]


def _scores(q_list, k):
    out = []
    for qt, u, _ in q_list:
        s = jnp.dot(k[:, u * HEAD_DIM:(u + 1) * HEAD_DIM], qt, preferred_element_type=F32)
        out.append((s, _fold_rows(s, jnp.maximum)))
    return out


def _softmax_pv(stats, scores, q_list, vt, acc_ref):
    probs = []
    for (m, l), (s, smax) in zip(stats, scores):
        m_new = jnp.maximum(m, jnp.max(smax, axis=0, keepdims=True))
        alpha = jnp.exp2(m - m_new)
        p = jnp.exp2(s - m_new)
        l = alpha * l + jnp.sum(_fold_rows(p, jnp.add), axis=0, keepdims=True)
        probs.append((m_new, l, alpha, p.astype(BF16)))
    new = []
    for c, ((m_new, l, alpha, p), (_, _, v_unit)) in enumerate(zip(probs, q_list)):
        vv = vt[v_unit * HEAD_DIM:(v_unit + 1) * HEAD_DIM, :]
        acc_ref[c] = alpha * acc_ref[c] + jnp.dot(vv, p, preferred_element_type=F32)
        new.append((m_new, l))
    return tuple(new)


def _attn_chains(q_list, sources, tq, tk, s_refs, mx_refs, acc_ref):
    n_ch = len(q_list)
    acc_ref[...] = jnp.zeros_like(acc_ref)
    stats = tuple((jnp.full((1, tq), NEG_BIG, F32), jnp.zeros((1, tq), F32)) for _ in q_list)
    kc_ref, vtc_ref = sources[0]
    if len(sources) == 2:
        k_ref, vt_ref = sources[1]
        n = k_ref.shape[0] // tk
        assert n >= 2 and n % 2 == 0 and n * tk == k_ref.shape[0]

        def keys_of(c):
            return pl.ds(c * tk if isinstance(c, int) else pl.multiple_of(c * tk, tk), tk)

        def put(slot, c):
            for i, (s, smax) in enumerate(_scores(q_list, k_ref[keys_of(c), :])):
                s_refs[slot][i] = s
                mx_refs[slot][i] = smax

        def take(st, slot, c):
            return _softmax_pv(st, [(s_refs[slot][i], mx_refs[slot][i]) for i in range(n_ch)], q_list,
                               vt_ref[:, keys_of(c)], acc_ref)

        put(0, 0)

        def body(j, st):
            a = 2 * j
            put(1, a + 1)
            st = take(st, 0, a)
            put(0, a + 2)
            return take(st, 1, a + 1)

        stats = lax.fori_loop(0, n // 2 - 1, body, stats)
        put(1, n - 1)
        stats = take(stats, 0, n - 2)
        s_ctx = _scores(q_list, kc_ref[...])
        stats = take(stats, 1, n - 1)
    else:
        s_ctx = _scores(q_list, kc_ref[...])
    stats = _softmax_pv(stats, s_ctx, q_list, vtc_ref[...], acc_ref)
    return [(acc_ref[c] / l).T for c, (_, l) in enumerate(stats)]


def _attn_scratch(n_chains, tq, tk):
    scores = pltpu.VMEM((n_chains, tk, tq), F32)
    maxima = pltpu.VMEM((n_chains, 8, tq), F32)
    return [scores, scores, maxima, maxima, pltpu.VMEM((n_chains, HEAD_DIM, tq), F32)]


def _transposed(q):
    return q.astype(F32).T.astype(BF16)


def _gqa_kernel(*refs, group, tk, n_sources):
    q_ref = refs[0]
    kv = refs[1:1 + 2 * n_sources]
    o_ref, s0_ref, s1_ref, mx0_ref, mx1_ref, acc_ref = refs[1 + 2 * n_sources:]
    sources = [(kv[2 * s], kv[2 * s + 1]) for s in range(n_sources)]
    tq = q_ref.shape[0]
    q_list = [(_transposed(q_ref[:, g * HEAD_DIM:(g + 1) * HEAD_DIM]), 0, 0) for g in range(group)]
    outs = _attn_chains(q_list, sources, tq, tk, (s0_ref, s1_ref), (mx0_ref, mx1_ref), acc_ref)
    for g, o in enumerate(outs):
        o_ref[:, g * HEAD_DIM:(g + 1) * HEAD_DIM] = o.astype(BF16)


def _gqa_attention(q_arr, q_col, kv_arrs, k_col, vt_unit, *, batch, sq, n_kv, group, tq, tk, name):
    nq = sq // tq
    gw = group * HEAD_DIM
    assert q_col % group == 0
    in_specs = [pl.BlockSpec((tq, gw), lambda b, h, i: (b * nq + i, q_col // group + h))]
    args = [q_arr]
    for k_arr, vt_arr, t in kv_arrs:
        in_specs.append(pl.BlockSpec((t, HEAD_DIM), lambda b, h, i: (b, k_col + h)))
        in_specs.append(pl.BlockSpec((None, HEAD_DIM, t), lambda b, h, i: (b, vt_unit + h, 0)))
        args += [k_arr, vt_arr]
    kern = functools.partial(_gqa_kernel, group=group, tk=tk, n_sources=len(kv_arrs))
    return pl.pallas_call(
        kern,
        grid=(batch, n_kv, nq),
        in_specs=in_specs,
        out_specs=pl.BlockSpec((tq, gw), lambda b, h, i: (b * nq + i, h)),
        out_shape=jax.ShapeDtypeStruct((batch * sq, n_kv * gw), BF16),
        scratch_shapes=_attn_scratch(group, tq, tk),
        compiler_params=_cparams(("parallel", "parallel", "parallel")),
        name=name,
    )(*args)


def _diff_kernel(*refs, hps, tk, lam_init, n_sources):
    q_ref, lq1, lk1, lq2, lk2, g_ref = refs[:6]
    kv = refs[6:6 + 2 * n_sources]
    o_ref, s0_ref, s1_ref, mx0_ref, mx1_ref, acc_ref = refs[6 + 2 * n_sources:]
    sources = [(kv[2 * s], kv[2 * s + 1]) for s in range(n_sources)]
    tq = q_ref.shape[0]
    half = HEAD_DIM // 2
    lam = (jnp.exp(jnp.sum(lq1[...] * lk1[...], axis=-1, keepdims=True))
           - jnp.exp(jnp.sum(lq2[...] * lk2[...], axis=-1, keepdims=True)) + lam_init)
    dim = lax.broadcasted_iota(jnp.int32, (HEAD_DIM, tq), 0)
    q_list = []
    for h in range(hps):
        qt = _transposed(q_ref[:, h * HEAD_DIM:(h + 1) * HEAD_DIM])
        zero = jnp.zeros_like(qt)
        q_list.append((jnp.where(dim < half, qt, zero), h, h))
        q_list.append((jnp.where(dim < half, zero, qt), h, h))
    outs = _attn_chains(q_list, sources, tq, tk, (s0_ref, s1_ref), (mx0_ref, mx1_ref), acc_ref)
    for h in range(hps):
        o = outs[2 * h] - lam * outs[2 * h + 1]
        o = o * lax.rsqrt(jnp.mean(o * o, axis=-1, keepdims=True) + EPS) * g_ref[...]
        o_ref[:, h * HEAD_DIM:(h + 1) * HEAD_DIM] = (o * (1.0 - lam_init)).astype(BF16)


def _diff_attention(q_arr, kv_arrs, lams, g_diff, *, batch, sq, heads, q_col, k_col, vt_unit,
                    hps, tq, tk, lam_init, name):
    nq = sq // tq
    half = HEAD_DIM // 2
    w = hps * HEAD_DIM
    assert heads % hps == 0 and q_col % hps == 0 and k_col % hps == 0 and vt_unit % hps == 0
    small = lambda width: pl.BlockSpec((1, width), lambda b, h, i: (0, 0))
    in_specs = [pl.BlockSpec((tq, w), lambda b, h, i: (b * nq + i, q_col // hps + h)),
                small(half), small(half), small(half), small(half), small(HEAD_DIM)]
    args = [q_arr, *lams, g_diff]
    for k_arr, vt_arr, t in kv_arrs:
        in_specs.append(pl.BlockSpec((t, w), lambda b, h, i: (b, k_col // hps + h)))
        in_specs.append(pl.BlockSpec((None, w, t), lambda b, h, i: (b, vt_unit // hps + h, 0)))
        args += [k_arr, vt_arr]
    kern = functools.partial(_diff_kernel, hps=hps, tk=tk, lam_init=lam_init, n_sources=len(kv_arrs))
    return pl.pallas_call(
        kern,
        grid=(batch, heads // hps, nq),
        in_specs=in_specs,
        out_specs=pl.BlockSpec((tq, w), lambda b, h, i: (b * nq + i, h)),
        out_shape=jax.ShapeDtypeStruct((batch * sq, heads * HEAD_DIM), BF16),
        scratch_shapes=_attn_scratch(2 * hps, tq, tk),
        compiler_params=_cparams(("parallel", "parallel", "parallel")),
        name=name,
    )(*args)


def _na_plan(seq, tq, wr_max, win_rows):
    rows_n = seq // GRID_W
    wr = min(wr_max, rows_n)
    r_per = tq // GRID_W
    outside = 2 * wr_max - 1
    starts, pats = [], []
    for blk in range(seq // tq):
        rf = blk * r_per
        lo = int(np.clip(rf - wr // 2, 0, rows_n - wr))
        w0 = min(lo, rows_n - win_rows)
        r = rf + np.arange(r_per)
        kr = w0 + np.arange(win_rows)
        r0 = np.clip(r - wr // 2, 0, rows_n - wr)
        row_ok = (kr[None, :] >= r0[:, None]) & (kr[None, :] < r0[:, None] + wr)
        roff = kr[None, :] - r[:, None] + wr_max - 1
        starts.append(w0 * GRID_W)
        pats.append(np.where(row_ok, roff, outside))
    uniq, pids = [], []
    for p in pats:
        for j, u in enumerate(uniq):
            if np.array_equal(u, p):
                pids.append(j)
                break
        else:
            pids.append(len(uniq))
            uniq.append(p)
    return np.asarray(starts, np.int32), np.asarray(pids, np.int32), np.stack(uniq)


def _na_bias(rpb_l, row_pats, tq, wk):
    heads = rpb_l.shape[0]
    cidx = np.arange(GRID_W)
    c0 = np.clip(cidx - NA_COLS // 2, 0, GRID_W - NA_COLS)
    col_ok = (cidx[None, :] >= c0[:, None]) & (cidx[None, :] < c0[:, None] + NA_COLS)
    coff = np.clip(cidx[None, :] - cidx[:, None], -(NA_COLS - 1), NA_COLS - 1) + NA_COLS - 1
    by_col = jnp.where(col_ok[None, None], rpb_l.astype(F32)[:, :, coff] * LOG2E, NEG_BIG)
    outside = jnp.full((heads, GRID_W, GRID_W), NEG_BIG, F32)
    n_rel = by_col.shape[1]
    tile = lambda rel: by_col[:, rel] if rel < n_rel else outside
    strips = [jnp.concatenate([tile(int(rel)) for rel in q_row], axis=-1)
              for pat in row_pats for q_row in pat]
    n_pat = row_pats.shape[0]
    return jnp.stack(strips, axis=1).reshape(heads, n_pat, tq, wk)


def _na_kernel(w0_ref, pid_ref, q_ref, kc_ref, vc_ref, k_ref, v_ref, bias_ref, o_ref, *, heads, wk):
    del pid_ref
    w0 = pl.multiple_of(w0_ref[pl.program_id(1)], GRID_W)
    win = pl.ds(w0, wk)
    for h in range(heads):
        lanes = slice(h * HEAD_DIM, (h + 1) * HEAD_DIM)
        q = q_ref[:, lanes]
        s_w = lax.dot_general(q, k_ref[win, lanes], (((1,), (1,)), ((), ())),
                              preferred_element_type=F32) + bias_ref[h]
        s_c = lax.dot_general(q, kc_ref[:, lanes], (((1,), (1,)), ((), ())), preferred_element_type=F32)
        m = jnp.maximum(jnp.max(s_w, axis=-1, keepdims=True), jnp.max(s_c, axis=-1, keepdims=True))
        p_w = jnp.exp2(s_w - m)
        p_c = jnp.exp2(s_c - m)
        l = jnp.sum(p_w, axis=-1, keepdims=True) + jnp.sum(p_c, axis=-1, keepdims=True)
        acc = (jnp.dot(p_w.astype(BF16), v_ref[win, lanes], preferred_element_type=F32)
               + jnp.dot(p_c.astype(BF16), vc_ref[:, lanes], preferred_element_type=F32))
        o_ref[:, lanes] = (acc / l).astype(BF16)


def _na_attention(qkv, qkv_c, rpb_l, *, batch, seq, ctx_len, heads, q_col, k_col, v_col, tq):
    wr_max = (rpb_l.shape[1] + 1) // 2
    r_per = tq // GRID_W
    win_rows = min(-(-(r_per + wr_max - 1) // 4) * 4, seq // GRID_W)
    wk = win_rows * GRID_W
    starts, pids, row_pats = _na_plan(seq, tq, wr_max, win_rows)
    bias = _na_bias(rpb_l, row_pats, tq, wk)
    nq = seq // tq
    w = heads * HEAD_DIM
    kern = functools.partial(_na_kernel, heads=heads, wk=wk)
    grid_spec = pltpu.PrefetchScalarGridSpec(
        num_scalar_prefetch=2,
        grid=(batch, nq),
        in_specs=[
            pl.BlockSpec((tq, w), lambda b, i, w0, pid: (b * nq + i, q_col // heads)),
            pl.BlockSpec((ctx_len, w), lambda b, i, w0, pid: (b, k_col // heads)),
            pl.BlockSpec((ctx_len, w), lambda b, i, w0, pid: (b, v_col // heads)),
            pl.BlockSpec((seq, w), lambda b, i, w0, pid: (b, k_col // heads)),
            pl.BlockSpec((seq, w), lambda b, i, w0, pid: (b, v_col // heads)),
            pl.BlockSpec((heads, None, tq, wk), lambda b, i, w0, pid: (0, pid[i], 0, 0)),
        ],
        out_specs=pl.BlockSpec((tq, w), lambda b, i, w0, pid: (b * nq + i, 0)),
    )
    return pl.pallas_call(
        kern,
        grid_spec=grid_spec,
        out_shape=jax.ShapeDtypeStruct((batch * seq, w), BF16),
        compiler_params=_cparams(("parallel", "parallel")),
        name="na_attention",
    )(jnp.asarray(starts), jnp.asarray(pids), qkv, qkv_c, qkv_c, qkv, qkv, bias)


def _dense_ctx_kernel(q_ref, k_ref, v_ref, o_ref, *, heads):
    for h in range(heads):
        lanes = slice(h * HEAD_DIM, (h + 1) * HEAD_DIM)
        s = lax.dot_general(q_ref[:, lanes], k_ref[:, lanes], (((1,), (1,)), ((), ())),
                            preferred_element_type=F32)
        p = jnp.exp2(s - jnp.max(s, axis=-1, keepdims=True))
        acc = jnp.dot(p.astype(BF16), v_ref[:, lanes], preferred_element_type=F32)
        o_ref[:, lanes] = (acc / jnp.sum(p, axis=-1, keepdims=True)).astype(BF16)


def _dense_ctx_attention(qkv_c, *, batch, ctx_len, heads, q_col, k_col, v_col):
    w = heads * HEAD_DIM
    spec = lambda col: pl.BlockSpec((ctx_len, w), lambda b: (b, col // heads))
    return pl.pallas_call(
        functools.partial(_dense_ctx_kernel, heads=heads),
        grid=(batch,),
        in_specs=[spec(q_col), spec(k_col), spec(v_col)],
        out_specs=pl.BlockSpec((ctx_len, w), lambda b: (b, 0)),
        out_shape=jax.ShapeDtypeStruct((batch * ctx_len, w), BF16),
        compiler_params=_cparams(("parallel",)),
        name="dense_attention_ctx",
    )(qkv_c, qkv_c, qkv_c)


def _outproj_kernel(oa_ref, ob_ref, on_ref, w_ref, h_ref, mod_ref, g_ref, o_ref, mix_ref):
    wa, wb = oa_ref.shape[1], ob_ref.shape[1]
    mix_ref[:, :wa] = oa_ref[...]
    mix_ref[:, wa:wa + wb] = ob_ref[...]
    mix_ref[:, wa + wb:] = on_ref[...]
    y = jnp.dot(mix_ref[...], w_ref[...], preferred_element_type=F32)
    yn = y * lax.rsqrt(jnp.mean(y * y, axis=-1, keepdims=True) + EPS) * g_ref[...]
    o_ref[...] = h_ref[...] + mod_ref[2:3] * yn


def _out_projection(oa, ob, on, w_bf16, h2d, mods, mod_row_of_block, g_post, *, layer, tm):
    rows, d = h2d.shape
    kdim = w_bf16.shape[1]
    return pl.pallas_call(
        _outproj_kernel,
        grid=(rows // tm,),
        in_specs=[
            pl.BlockSpec((tm, oa.shape[1]), lambda i: (i, 0)),
            pl.BlockSpec((tm, ob.shape[1]), lambda i: (i, 0)),
            pl.BlockSpec((tm, on.shape[1]), lambda i: (i, 0)),
            pl.BlockSpec((None, kdim, d), lambda i: (layer, 0, 0), pipeline_mode=pl.Buffered(1)),
            pl.BlockSpec((tm, d), lambda i: (i, 0)),
            pl.BlockSpec((None, 6, d), lambda i: (mod_row_of_block(i), 0, 0)),
            pl.BlockSpec((1, d), lambda i: (0, 0)),
        ],
        out_specs=pl.BlockSpec((tm, d), lambda i: (i, 0)),
        out_shape=jax.ShapeDtypeStruct((rows, d), F32),
        scratch_shapes=[pltpu.VMEM((tm, kdim), BF16)],
        compiler_params=_cparams(("parallel",)),
        name="out_projection",
    )(oa, ob, on, w_bf16, h2d, mods, g_post)


def _ffn_kernel(h_ref, mod_ref, gpre_ref, wg_ref, wu_ref, wd_ref, gpost_ref, o_ref, xm_ref):
    f = pl.program_id(1)
    tm = o_ref.shape[0]
    slab = min(tm, HEAD_DIM)

    def for_slabs(fn):
        def body(r, carry):
            fn(pl.ds(pl.multiple_of(r * slab, slab), slab))
            return carry
        lax.fori_loop(0, tm // slab, body, 0)

    @pl.when(f == 0)
    def _():
        def prologue(rows):
            ms = jnp.mean(jnp.square(h_ref[rows, :]), axis=-1, keepdims=True)
            xn = h_ref[rows, :] * lax.rsqrt(ms + EPS) * gpre_ref[...]
            xm_ref[rows, :] = (xn * (1.0 + mod_ref[4:5]) + mod_ref[3:4]).astype(BF16)
            o_ref[rows, :] = jnp.zeros((slab, o_ref.shape[1]), F32)
        for_slabs(prologue)

    xm = xm_ref[...]
    gate = jnp.dot(xm, wg_ref[...], preferred_element_type=F32)
    up = jnp.dot(xm, wu_ref[...], preferred_element_type=F32)
    act = (gate * jax.nn.sigmoid(gate) * up).astype(BF16)
    o_ref[...] += jnp.dot(act, wd_ref[...], preferred_element_type=F32)

    @pl.when(f == pl.num_programs(1) - 1)
    def _():
        def epilogue(rows):
            ms = jnp.mean(jnp.square(o_ref[rows, :]), axis=-1, keepdims=True)
            yn = o_ref[rows, :] * lax.rsqrt(ms + EPS) * gpost_ref[...]
            o_ref[rows, :] = h_ref[rows, :] + mod_ref[5:6] * yn
        for_slabs(epilogue)


def _ffn(h2d, mods, mod_row_of_block, g_pre, wg, wu, wd, g_post, *, layer, tm, tf):
    rows, d = h2d.shape
    ff = wg.shape[2]
    return pl.pallas_call(
        _ffn_kernel,
        grid=(rows // tm, ff // tf),
        in_specs=[
            pl.BlockSpec((tm, d), lambda i, f: (i, 0)),
            pl.BlockSpec((None, 6, d), lambda i, f: (mod_row_of_block(i), 0, 0)),
            pl.BlockSpec((1, d), lambda i, f: (0, 0)),
            pl.BlockSpec((None, d, tf), lambda i, f: (layer, 0, f)),
            pl.BlockSpec((None, d, tf), lambda i, f: (layer, 0, f)),
            pl.BlockSpec((None, tf, d), lambda i, f: (layer, f, 0)),
            pl.BlockSpec((1, d), lambda i, f: (0, 0)),
        ],
        out_specs=pl.BlockSpec((tm, d), lambda i, f: (i, 0)),
        out_shape=jax.ShapeDtypeStruct((rows, d), F32),
        scratch_shapes=[pltpu.VMEM((tm, d), BF16)],
        compiler_params=_cparams(("parallel", "arbitrary")),
        name="ffn",
    )(h2d, mods, g_pre, wg, wu, wd, g_post)


def kernel(x, c, ctx, c_ctx, w_mod, b_mod, g_pre1, g_post1, g_pre2, g_post2, w_in, w_out,
           lam_q1, lam_k1, lam_q2, lam_k2, g_diff, g_qn, g_kn, rpb, w_gate, w_up, w_down):
    batch, seq, d = x.shape
    ctx_len = ctx.shape[1]
    depth = w_mod.shape[0]
    n_heads = d // HEAD_DIM
    a_heads, b_heads, c_heads = n_heads // 4, n_heads // 2, n_heads // 4
    b_kv = b_heads // 4
    group = b_heads // b_kv
    a_qk = a_v = a_heads * HEAD_DIM
    b_q, b_kvw, c_w = b_heads * HEAD_DIM, b_kv * HEAD_DIM, c_heads * HEAD_DIM
    sizes = (a_qk, a_qk, a_v, b_q, b_kvw, b_kvw, c_w, c_w, c_w)
    starts = np.cumsum((0,) + sizes)[:-1] // HEAD_DIM
    qa_c, ka_c, _, qb_c, kb_c, _, qn_c, kn_c, vn_c = (int(s) for s in starts)
    vta_u, vtb_u = 0, a_heads
    assert batch + 1 <= MOD_ROWS and seq % GRID_W == 0

    tm = _pick(seq, (512, 256, 128))
    tm_c = _pick(ctx_len, (512, 256, 128))
    tq = _pick(seq, (256, 128))
    tq_c = _pick(ctx_len, (256, 128))
    tk = 512
    tf = _pick(w_gate.shape[2], (512, 256, 128))
    tm_ffn = _pick(seq, (1024, 512, 256, 128))
    tm_ffn_c = _pick(batch * ctx_len, (1024, 512, 256, 128))
    hps = 2 if a_heads % 2 == 0 else 1

    c_rows = jnp.zeros((MOD_ROWS, d), F32).at[:batch].set(c).at[batch].set(c_ctx)
    mods = _mod_vectors(c_rows, w_mod, b_mod).reshape(depth, MOD_ROWS, 6, d)
    tabs_a = _rope_tables(seq, HEAD_DIM // 2)
    tabs_b = _rope_tables(seq, HEAD_DIM)
    tabs_c = jnp.zeros((3, tm_c, HEAD_DIM), F32)

    lat_row = lambda i: i // (seq // tm)
    ctx_row = lambda i: batch
    row1 = lambda v: v.reshape(1, -1)

    w_in_b, w_out_b = _to_bf16(w_in), _to_bf16(w_out)
    wg, wu, wd = _to_bf16(w_gate), _to_bf16(w_up), _to_bf16(w_down)

    h = x.reshape(batch * seq, d)
    hc = ctx.reshape(batch * ctx_len, d)
    for l in range(depth):
        need_ctx = l < depth - 1
        lam_init = 0.8 - 0.6 * math.exp(-0.3 * l)
        lams = [row1(v[l]) for v in (lam_q1, lam_k1, lam_q2, lam_k2)]
        inproj = functools.partial(_in_projection, g_pre=row1(g_pre1[l]), w_bf16=w_in_b, layer=l,
                                   g_qn=row1(g_qn[l]), g_kn=row1(g_kn[l]), sizes=sizes, batch=batch)
        qkv, vt = inproj(h, mods[l], lat_row, tabs_a=tabs_a, tabs_b=tabs_b, seq=seq, tm=tm, use_rope=True)
        qkv_c, vt_c = inproj(hc, mods[l], ctx_row, tabs_a=tabs_c, tabs_b=tabs_c, seq=ctx_len, tm=tm_c,
                             use_rope=False)
        both = [(qkv_c, vt_c, ctx_len), (qkv, vt, seq)]
        only_ctx = [(qkv_c, vt_c, ctx_len)]

        oa = _diff_attention(qkv, both, lams, row1(g_diff[l]), batch=batch, sq=seq, heads=a_heads,
                             q_col=qa_c, k_col=ka_c, vt_unit=vta_u, hps=hps, tq=tq, tk=tk,
                             lam_init=lam_init, name="diff_attention")
        ob = _gqa_attention(qkv, qb_c, both, kb_c, vtb_u, batch=batch, sq=seq, n_kv=b_kv, group=group,
                            tq=tq, tk=tk, name="gqa_attention")
        on = _na_attention(qkv, qkv_c, rpb[l], batch=batch, seq=seq, ctx_len=ctx_len, heads=c_heads,
                           q_col=qn_c, k_col=kn_c, v_col=vn_c, tq=tq)
        h_mid = _out_projection(oa, ob, on, w_out_b, h, mods[l], lat_row, row1(g_post1[l]), layer=l, tm=tm)
        h_new = _ffn(h_mid, mods[l], lambda i: i // (seq // tm_ffn), row1(g_pre2[l]), wg, wu, wd,
                     row1(g_post2[l]), layer=l, tm=tm_ffn, tf=tf)

        if need_ctx:
            oac = _diff_attention(qkv_c, only_ctx, lams, row1(g_diff[l]), batch=batch, sq=ctx_len,
                                  heads=a_heads, q_col=qa_c, k_col=ka_c, vt_unit=vta_u, hps=hps,
                                  tq=tq_c, tk=tk, lam_init=lam_init, name="diff_attention_ctx")
            obc = _gqa_attention(qkv_c, qb_c, only_ctx, kb_c, vtb_u, batch=batch, sq=ctx_len, n_kv=b_kv,
                                 group=group, tq=tq_c, tk=tk, name="gqa_attention_ctx")
            onc = _dense_ctx_attention(qkv_c, batch=batch, ctx_len=ctx_len, heads=c_heads, q_col=qn_c,
                                       k_col=kn_c, v_col=vn_c)
            hc_mid = _out_projection(oac, obc, onc, w_out_b, hc, mods[l], ctx_row, row1(g_post1[l]),
                                     layer=l, tm=tm_c)
            hc = _ffn(hc_mid, mods[l], ctx_row, row1(g_pre2[l]), wg, wu, wd, row1(g_post2[l]),
                      layer=l, tm=tm_ffn_c, tf=tf)
        h = h_new
    return h.reshape(batch, seq, d)
```

```python
import functools
import math

import numpy as np
import jax
import jax.numpy as jnp
from jax import lax
from jax.experimental import pallas as pl
from jax.experimental.pallas import tpu as pltpu

GRID_W = 64
HEAD_DIM = 128
NA_COLS = 16
ROPE_THETA = 10000.0
EPS = 1e-6
NEG_BIG = -1e30
LOG2E = math.log2(math.e)
MOD_ROWS = 8
V7X_VMEM_LIMIT = 56 * 1024 * 1024

F32 = jnp.float32
BF16 = jnp.bfloat16


def _cparams(sem):
    return pltpu.CompilerParams(dimension_semantics=sem, vmem_limit_bytes=V7X_VMEM_LIMIT)


def _pick(n, candidates):
    for c in candidates:
        if n % c == 0:
            return c
    raise ValueError(f"no tile in {candidates} divides {n}")


def _mod_kernel(c_ref, w_ref, b_ref, o_ref):
    c = c_ref[...]
    a = (c * jax.nn.sigmoid(c)).astype(BF16)
    o_ref[0] = jnp.dot(a, w_ref[0].astype(BF16), preferred_element_type=F32) + b_ref[0]


def _mod_vectors(c_rows, w_mod, b_mod):
    depth, d, n = w_mod.shape
    tn = _pick(n, (1024, 512, 256, 128))
    return pl.pallas_call(
        _mod_kernel,
        grid=(depth, n // tn),
        in_specs=[
            pl.BlockSpec((MOD_ROWS, d), lambda l, j: (0, 0)),
            pl.BlockSpec((1, d, tn), lambda l, j: (l, 0, j)),
            pl.BlockSpec((1, 1, tn), lambda l, j: (l, 0, j)),
        ],
        out_specs=pl.BlockSpec((1, MOD_ROWS, tn), lambda l, j: (l, 0, j)),
        out_shape=jax.ShapeDtypeStruct((depth, MOD_ROWS, n), F32),
        compiler_params=_cparams(("parallel", "parallel")),
        name="mod_vectors",
    )(c_rows, w_mod, b_mod.reshape(depth, 1, n))


def _cast_kernel(w_ref, o_ref):
    o_ref[...] = w_ref[...].astype(BF16)


def _to_bf16(w):
    depth, rows, cols = w.shape
    tr = _pick(rows, (256, 128))
    return pl.pallas_call(
        _cast_kernel,
        grid=(depth, rows // tr),
        in_specs=[pl.BlockSpec((None, tr, cols), lambda l, i: (l, i, 0))],
        out_specs=pl.BlockSpec((None, tr, cols), lambda l, i: (l, i, 0)),
        out_shape=jax.ShapeDtypeStruct(w.shape, BF16),
        compiler_params=_cparams(("parallel", "parallel")),
        name="weights_to_bf16",
    )(w)


def _rope_tables(seq, head_width):
    quarter = head_width // 4
    lane = np.arange(HEAD_DIM)
    within_head = lane % head_width
    use_col = (within_head // (2 * quarter)) == 1
    within_half = within_head % (2 * quarter)
    first = within_half < quarter
    freq = within_half % quarter
    t = jnp.arange(seq)
    row, col = t // GRID_W, t % GRID_W
    inv = (ROPE_THETA ** (-jnp.arange(quarter, dtype=F32) / quarter))[freq]
    pos = jnp.where(use_col[None, :], col.astype(F32)[:, None], row.astype(F32)[:, None])
    ang = pos * inv[None, :]
    cos, sin = jnp.cos(ang), jnp.sin(ang)
    zero = jnp.zeros_like(sin)
    return jnp.stack([cos, jnp.where(first[None, :], -sin, zero), jnp.where(first[None, :], zero, sin)])


def _rope(y, tab_ref, quarter):
    up = pltpu.roll(y, HEAD_DIM - quarter, 1)
    down = pltpu.roll(y, quarter, 1)
    return y * tab_ref[0] + up * tab_ref[1] + down * tab_ref[2]


def _col_plan(sizes):
    a_qk, _, a_v, b_q, b_kv, _, c_w, _, _ = sizes
    u = lambda width: width // HEAD_DIM
    s_a = (HEAD_DIM // 2) ** -0.5 * LOG2E
    s_b = HEAD_DIM ** -0.5 * LOG2E
    t_qb, t_va, t_vb = u(a_qk), u(a_qk) + u(b_q), u(a_qk) + u(b_q) + u(a_v)
    plan = ([("a", None, s_a, i) for i in range(u(a_qk))] + [("a", None, None, None)] * u(a_qk)
            + [(None, None, None, t_va + i) for i in range(u(a_v))]
            + [("b", "q", s_b, t_qb + i) for i in range(u(b_q))] + [("b", "k", None, None)] * u(b_kv)
            + [(None, None, None, t_vb + i) for i in range(u(b_kv))]
            + [(None, None, s_b, None)] * u(c_w) + [(None, None, None, None)] * (2 * u(c_w)))
    return plan


def _inproj_kernel(x_ref, mod_ref, g_ref, w_ref, gq_ref, gk_ref, ta_ref, tb_ref, o_ref, vt_ref, *,
                   plan, chunk_units, use_rope):
    x = x_ref[...]
    m = mod_ref[...]
    ms = jnp.mean(x * x, axis=-1, keepdims=True)
    xn = x * lax.rsqrt(ms + EPS) * g_ref[...]
    xm = (xn * (1.0 + m[1:2]) + m[0:1]).astype(BF16)
    n_units = len(plan)
    for u0 in range(0, n_units, chunk_units):
        u1 = min(u0 + chunk_units, n_units)
        y = jnp.dot(xm, w_ref[:, u0 * HEAD_DIM:u1 * HEAD_DIM], preferred_element_type=F32)
        for u in range(u0, u1):
            yu = y[:, (u - u0) * HEAD_DIM:(u - u0 + 1) * HEAD_DIM]
            rope, norm, q_scale, vt_slot = plan[u]
            if norm is not None:
                gain = gq_ref[...] if norm == "q" else gk_ref[...]
                yu = yu * lax.rsqrt(jnp.mean(yu * yu, axis=-1, keepdims=True) + EPS) * gain
            if rope is not None and use_rope:
                yu = _rope(yu, ta_ref, HEAD_DIM // 8) if rope == "a" else _rope(yu, tb_ref, HEAD_DIM // 4)
            if q_scale is not None:
                yu = yu * q_scale
            o_ref[:, u * HEAD_DIM:(u + 1) * HEAD_DIM] = yu.astype(BF16)
            if vt_slot is not None:
                vt_ref[vt_slot * HEAD_DIM:(vt_slot + 1) * HEAD_DIM, :] = yu.T.astype(BF16)


def _in_projection(h2d, mods, mod_row_of_block, g_pre, w_bf16, g_qn, g_kn, tabs_a, tabs_b, *,
                   layer, sizes, batch, seq, tm, use_rope):
    rows, d = h2d.shape
    n = w_bf16.shape[2]
    plan = _col_plan(sizes)
    n_vt = sum(1 for p in plan if p[3] is not None)
    bps = seq // tm
    kern = functools.partial(_inproj_kernel, plan=plan, chunk_units=4, use_rope=use_rope)
    return pl.pallas_call(
        kern,
        grid=(rows // tm,),
        in_specs=[
            pl.BlockSpec((tm, d), lambda i: (i, 0)),
            pl.BlockSpec((None, 6, d), lambda i: (mod_row_of_block(i), 0, 0)),
            pl.BlockSpec((1, d), lambda i: (0, 0)),
            pl.BlockSpec((None, d, n), lambda i: (layer, 0, 0), pipeline_mode=pl.Buffered(1)),
            pl.BlockSpec((1, HEAD_DIM), lambda i: (0, 0)),
            pl.BlockSpec((1, HEAD_DIM), lambda i: (0, 0)),
            pl.BlockSpec((3, tm, HEAD_DIM), lambda i: (0, i % bps, 0)),
            pl.BlockSpec((3, tm, HEAD_DIM), lambda i: (0, i % bps, 0)),
        ],
        out_specs=[pl.BlockSpec((tm, n), lambda i: (i, 0)),
                   pl.BlockSpec((None, n_vt * HEAD_DIM, tm), lambda i: (i // bps, 0, i % bps))],
        out_shape=[jax.ShapeDtypeStruct((rows, n), BF16),
                   jax.ShapeDtypeStruct((batch, n_vt * HEAD_DIM, seq), BF16)],
        compiler_params=_cparams(("parallel",)),
        name="in_projection",
    )(h2d, mods, g_pre, w_bf16, g_qn, g_kn, tabs_a, tabs_b)


def _fold_rows(x, op):
    slabs = [x[r:r + 8] for r in range(0, x.shape[0], 8)]
    while len(slabs) > 1:
        nxt = [op(slabs[i], slabs[i + 1]) for i in range(0, len(slabs) - 1, 2)]
        if len(slabs) % 2:
            nxt.append(slabs[-1])
        slabs = nxt
    return slabs[0]


def _scores(q_list, k):
    out = []
    for qt, u, _ in q_list:
        s = jnp.dot(k[:, u * HEAD_DIM:(u + 1) * HEAD_DIM], qt, preferred_element_type=F32)
        out.append((s, _fold_rows(s, jnp.maximum)))
    return out


def _softmax_pv(stats, scores, q_list, vt, acc_ref):
    probs = []
    for (m, l), (s, smax) in zip(stats, scores):
        m_new = jnp.maximum(m, jnp.max(smax, axis=0, keepdims=True))
        alpha = jnp.exp2(m - m_new)
        p = jnp.exp2(s - m_new)
        l = alpha * l + jnp.sum(_fold_rows(p, jnp.add), axis=0, keepdims=True)
        probs.append((m_new, l, alpha, p.astype(BF16)))
    new = []
    for c, ((m_new, l, alpha, p), (_, _, v_unit)) in enumerate(zip(probs, q_list)):
        vv = vt[v_unit * HEAD_DIM:(v_unit + 1) * HEAD_DIM, :]
        acc_ref[c] = alpha * acc_ref[c] + jnp.dot(vv, p, preferred_element_type=F32)
        new.append((m_new, l))
    return tuple(new)


def _attn_chains(q_list, sources, tq, tk, s_refs, mx_refs, acc_ref):
    n_ch = len(q_list)
    acc_ref[...] = jnp.zeros_like(acc_ref)
    stats = tuple((jnp.full((1, tq), NEG_BIG, F32), jnp.zeros((1, tq), F32)) for _ in q_list)
    kc_ref, vtc_ref = sources[0]
    if len(sources) == 2:
        k_ref, vt_ref = sources[1]
        n = k_ref.shape[0] // tk
        assert n >= 2 and n % 2 == 0 and n * tk == k_ref.shape[0]

        def keys_of(c):
            return pl.ds(c * tk if isinstance(c, int) else pl.multiple_of(c * tk, tk), tk)

        def put(slot, c):
            for i, (s, smax) in enumerate(_scores(q_list, k_ref[keys_of(c), :])):
                s_refs[slot][i] = s
                mx_refs[slot][i] = smax

        def take(st, slot, c):
            return _softmax_pv(st, [(s_refs[slot][i], mx_refs[slot][i]) for i in range(n_ch)], q_list,
                               vt_ref[:, keys_of(c)], acc_ref)

        put(0, 0)

        def body(j, st):
            a = 2 * j
            put(1, a + 1)
            st = take(st, 0, a)
            put(0, a + 2)
            return take(st, 1, a + 1)

        stats = lax.fori_loop(0, n // 2 - 1, body, stats)
        put(1, n - 1)
        stats = take(stats, 0, n - 2)
        s_ctx = _scores(q_list, kc_ref[...])
        stats = take(stats, 1, n - 1)
    else:
        s_ctx = _scores(q_list, kc_ref[...])
    stats = _softmax_pv(stats, s_ctx, q_list, vtc_ref[...], acc_ref)
    return [(acc_ref[c] / l).T for c, (_, l) in enumerate(stats)]


def _attn_scratch(n_chains, tq, tk):
    scores = pltpu.VMEM((n_chains, tk, tq), F32)
    maxima = pltpu.VMEM((n_chains, 8, tq), F32)
    return [scores, scores, maxima, maxima, pltpu.VMEM((n_chains, HEAD_DIM, tq), F32)]


def _gqa_kernel(*refs, group, tk, n_sources):
    qt_ref = refs[0]
    kv = refs[1:1 + 2 * n_sources]
    o_ref, s0_ref, s1_ref, mx0_ref, mx1_ref, acc_ref = refs[1 + 2 * n_sources:]
    sources = [(kv[2 * s], kv[2 * s + 1]) for s in range(n_sources)]
    tq = qt_ref.shape[1]
    q_list = [(qt_ref[g * HEAD_DIM:(g + 1) * HEAD_DIM, :], 0, 0) for g in range(group)]
    outs = _attn_chains(q_list, sources, tq, tk, (s0_ref, s1_ref), (mx0_ref, mx1_ref), acc_ref)
    for g, o in enumerate(outs):
        o_ref[:, g * HEAD_DIM:(g + 1) * HEAD_DIM] = o.astype(BF16)


def _gqa_attention(qt_arr, qt_unit, kv_arrs, k_col, vt_unit, *, batch, sq, n_kv, group, tq, tk, name):
    nq = sq // tq
    gw = group * HEAD_DIM
    assert qt_unit % group == 0
    in_specs = [pl.BlockSpec((None, gw, tq), lambda b, h, i: (b, qt_unit // group + h, i))]
    args = [qt_arr]
    for k_arr, vt_arr, t in kv_arrs:
        in_specs.append(pl.BlockSpec((t, HEAD_DIM), lambda b, h, i: (b, k_col + h)))
        in_specs.append(pl.BlockSpec((None, HEAD_DIM, t), lambda b, h, i: (b, vt_unit + h, 0)))
        args += [k_arr, vt_arr]
    kern = functools.partial(_gqa_kernel, group=group, tk=tk, n_sources=len(kv_arrs))
    return pl.pallas_call(
        kern,
        grid=(batch, n_kv, nq),
        in_specs=in_specs,
        out_specs=pl.BlockSpec((tq, gw), lambda b, h, i: (b * nq + i, h)),
        out_shape=jax.ShapeDtypeStruct((batch * sq, n_kv * gw), BF16),
        scratch_shapes=_attn_scratch(group, tq, tk),
        compiler_params=_cparams(("parallel", "parallel", "parallel")),
        name=name,
    )(*args)


def _diff_kernel(*refs, hps, tk, lam_init, n_sources):
    qt_ref, lq1, lk1, lq2, lk2, g_ref = refs[:6]
    kv = refs[6:6 + 2 * n_sources]
    o_ref, s0_ref, s1_ref, mx0_ref, mx1_ref, acc_ref = refs[6 + 2 * n_sources:]
    sources = [(kv[2 * s], kv[2 * s + 1]) for s in range(n_sources)]
    tq = qt_ref.shape[1]
    half = HEAD_DIM // 2
    lam = (jnp.exp(jnp.sum(lq1[...] * lk1[...], axis=-1, keepdims=True))
           - jnp.exp(jnp.sum(lq2[...] * lk2[...], axis=-1, keepdims=True)) + lam_init)
    dim = lax.broadcasted_iota(jnp.int32, (HEAD_DIM, tq), 0)
    q_list = []
    for h in range(hps):
        qt = qt_ref[h * HEAD_DIM:(h + 1) * HEAD_DIM, :]
        zero = jnp.zeros_like(qt)
        q_list.append((jnp.where(dim < half, qt, zero), h, h))
        q_list.append((jnp.where(dim < half, zero, qt), h, h))
    outs = _attn_chains(q_list, sources, tq, tk, (s0_ref, s1_ref), (mx0_ref, mx1_ref), acc_ref)
    for h in range(hps):
        o = outs[2 * h] - lam * outs[2 * h + 1]
        o = o * lax.rsqrt(jnp.mean(o * o, axis=-1, keepdims=True) + EPS) * g_ref[...]
        o_ref[:, h * HEAD_DIM:(h + 1) * HEAD_DIM] = (o * (1.0 - lam_init)).astype(BF16)


def _diff_attention(qt_arr, kv_arrs, lams, g_diff, *, batch, sq, heads, qt_unit, k_col, vt_unit,
                    hps, tq, tk, lam_init, name):
    nq = sq // tq
    half = HEAD_DIM // 2
    w = hps * HEAD_DIM
    assert heads % hps == 0 and qt_unit % hps == 0 and k_col % hps == 0 and vt_unit % hps == 0
    small = lambda width: pl.BlockSpec((1, width), lambda b, h, i: (0, 0))
    in_specs = [pl.BlockSpec((None, w, tq), lambda b, h, i: (b, qt_unit // hps + h, i)),
                small(half), small(half), small(half), small(half), small(HEAD_DIM)]
    args = [qt_arr, *lams, g_diff]
    for k_arr, vt_arr, t in kv_arrs:
        in_specs.append(pl.BlockSpec((t, w), lambda b, h, i: (b, k_col // hps + h)))
        in_specs.append(pl.BlockSpec((None, w, t), lambda b, h, i: (b, vt_unit // hps + h, 0)))
        args += [k_arr, vt_arr]
    kern = functools.partial(_diff_kernel, hps=hps, tk=tk, lam_init=lam_init, n_sources=len(kv_arrs))
    return pl.pallas_call(
        kern,
        grid=(batch, heads // hps, nq),
        in_specs=in_specs,
        out_specs=pl.BlockSpec((tq, w), lambda b, h, i: (b * nq + i, h)),
        out_shape=jax.ShapeDtypeStruct((batch * sq, heads * HEAD_DIM), BF16),
        scratch_shapes=_attn_scratch(2 * hps, tq, tk),
        compiler_params=_cparams(("parallel", "parallel", "parallel")),
        name=name,
    )(*args)


def _na_plan(seq, tq, wr_max, win_rows):
    rows_n = seq // GRID_W
    wr = min(wr_max, rows_n)
    r_per = tq // GRID_W
    outside = 2 * wr_max - 1
    starts, pats = [], []
    for blk in range(seq // tq):
        rf = blk * r_per
        lo = int(np.clip(rf - wr // 2, 0, rows_n - wr))
        w0 = min(lo, rows_n - win_rows)
        r = rf + np.arange(r_per)
        kr = w0 + np.arange(win_rows)
        r0 = np.clip(r - wr // 2, 0, rows_n - wr)
        row_ok = (kr[None, :] >= r0[:, None]) & (kr[None, :] < r0[:, None] + wr)
        roff = kr[None, :] - r[:, None] + wr_max - 1
        starts.append(w0 * GRID_W)
        pats.append(np.where(row_ok, roff, outside))
    uniq, pids = [], []
    for p in pats:
        for j, u in enumerate(uniq):
            if np.array_equal(u, p):
                pids.append(j)
                break
        else:
            pids.append(len(uniq))
            uniq.append(p)
    return np.asarray(starts, np.int32), np.asarray(pids, np.int32), np.stack(uniq)


def _na_bias(rpb_l, row_pats, tq, wk):
    heads = rpb_l.shape[0]
    cidx = np.arange(GRID_W)
    c0 = np.clip(cidx - NA_COLS // 2, 0, GRID_W - NA_COLS)
    col_ok = (cidx[None, :] >= c0[:, None]) & (cidx[None, :] < c0[:, None] + NA_COLS)
    coff = np.clip(cidx[None, :] - cidx[:, None], -(NA_COLS - 1), NA_COLS - 1) + NA_COLS - 1
    by_col = jnp.where(col_ok[None, None], rpb_l.astype(F32)[:, :, coff] * LOG2E, NEG_BIG)
    outside = jnp.full((heads, GRID_W, GRID_W), NEG_BIG, F32)
    n_rel = by_col.shape[1]
    tile = lambda rel: by_col[:, rel] if rel < n_rel else outside
    strips = [jnp.concatenate([tile(int(rel)) for rel in q_row], axis=-1)
              for pat in row_pats for q_row in pat]
    n_pat = row_pats.shape[0]
    return jnp.stack(strips, axis=1).reshape(heads, n_pat, tq, wk)


def _na_kernel(w0_ref, pid_ref, q_ref, kc_ref, vc_ref, k_ref, v_ref, bias_ref, o_ref, *, heads, wk):
    del pid_ref
    w0 = pl.multiple_of(w0_ref[pl.program_id(1)], GRID_W)
    win = pl.ds(w0, wk)
    for h in range(heads):
        lanes = slice(h * HEAD_DIM, (h + 1) * HEAD_DIM)
        q = q_ref[:, lanes]
        s_w = lax.dot_general(q, k_ref[win, lanes], (((1,), (1,)), ((), ())),
                              preferred_element_type=F32) + bias_ref[h]
        s_c = lax.dot_general(q, kc_ref[:, lanes], (((1,), (1,)), ((), ())), preferred_element_type=F32)
        m = jnp.maximum(jnp.max(s_w, axis=-1, keepdims=True), jnp.max(s_c, axis=-1, keepdims=True))
        p_w = jnp.exp2(s_w - m)
        p_c = jnp.exp2(s_c - m)
        l = jnp.sum(p_w, axis=-1, keepdims=True) + jnp.sum(p_c, axis=-1, keepdims=True)
        acc = (jnp.dot(p_w.astype(BF16), v_ref[win, lanes], preferred_element_type=F32)
               + jnp.dot(p_c.astype(BF16), vc_ref[:, lanes], preferred_element_type=F32))
        o_ref[:, lanes] = (acc / l).astype(BF16)


def _na_attention(qkv, qkv_c, rpb_l, *, batch, seq, ctx_len, heads, q_col, k_col, v_col, tq):
    wr_max = (rpb_l.shape[1] + 1) // 2
    r_per = tq // GRID_W
    win_rows = min(-(-(r_per + wr_max - 1) // 4) * 4, seq // GRID_W)
    wk = win_rows * GRID_W
    starts, pids, row_pats = _na_plan(seq, tq, wr_max, win_rows)
    bias = _na_bias(rpb_l, row_pats, tq, wk)
    nq = seq // tq
    w = heads * HEAD_DIM
    kern = functools.partial(_na_kernel, heads=heads, wk=wk)
    grid_spec = pltpu.PrefetchScalarGridSpec(
        num_scalar_prefetch=2,
        grid=(batch, nq),
        in_specs=[
            pl.BlockSpec((tq, w), lambda b, i, w0, pid: (b * nq + i, q_col // heads)),
            pl.BlockSpec((ctx_len, w), lambda b, i, w0, pid: (b, k_col // heads)),
            pl.BlockSpec((ctx_len, w), lambda b, i, w0, pid: (b, v_col // heads)),
            pl.BlockSpec((seq, w), lambda b, i, w0, pid: (b, k_col // heads)),
            pl.BlockSpec((seq, w), lambda b, i, w0, pid: (b, v_col // heads)),
            pl.BlockSpec((heads, None, tq, wk), lambda b, i, w0, pid: (0, pid[i], 0, 0)),
        ],
        out_specs=pl.BlockSpec((tq, w), lambda b, i, w0, pid: (b * nq + i, 0)),
    )
    return pl.pallas_call(
        kern,
        grid_spec=grid_spec,
        out_shape=jax.ShapeDtypeStruct((batch * seq, w), BF16),
        compiler_params=_cparams(("parallel", "parallel")),
        name="na_attention",
    )(jnp.asarray(starts), jnp.asarray(pids), qkv, qkv_c, qkv_c, qkv, qkv, bias)


def _dense_ctx_kernel(q_ref, k_ref, v_ref, o_ref, *, heads):
    for h in range(heads):
        lanes = slice(h * HEAD_DIM, (h + 1) * HEAD_DIM)
        s = lax.dot_general(q_ref[:, lanes], k_ref[:, lanes], (((1,), (1,)), ((), ())),
                            preferred_element_type=F32)
        p = jnp.exp2(s - jnp.max(s, axis=-1, keepdims=True))
        acc = jnp.dot(p.astype(BF16), v_ref[:, lanes], preferred_element_type=F32)
        o_ref[:, lanes] = (acc / jnp.sum(p, axis=-1, keepdims=True)).astype(BF16)


def _dense_ctx_attention(qkv_c, *, batch, ctx_len, heads, q_col, k_col, v_col):
    w = heads * HEAD_DIM
    spec = lambda col: pl.BlockSpec((ctx_len, w), lambda b: (b, col // heads))
    return pl.pallas_call(
        functools.partial(_dense_ctx_kernel, heads=heads),
        grid=(batch,),
        in_specs=[spec(q_col), spec(k_col), spec(v_col)],
        out_specs=pl.BlockSpec((ctx_len, w), lambda b: (b, 0)),
        out_shape=jax.ShapeDtypeStruct((batch * ctx_len, w), BF16),
        compiler_params=_cparams(("parallel",)),
        name="dense_attention_ctx",
    )(qkv_c, qkv_c, qkv_c)


def _outproj_kernel(oa_ref, ob_ref, on_ref, w_ref, h_ref, mod_ref, g_ref, o_ref):
    wa, wb = oa_ref.shape[1], ob_ref.shape[1]
    tm = o_ref.shape[0]
    slab = min(tm, 2 * HEAD_DIM)
    for r0 in range(0, tm, slab):
        rows = slice(r0, r0 + slab)
        y = jnp.dot(oa_ref[rows, :], w_ref[:wa, :], preferred_element_type=F32)
        y += jnp.dot(ob_ref[rows, :], w_ref[wa:wa + wb, :], preferred_element_type=F32)
        y += jnp.dot(on_ref[rows, :], w_ref[wa + wb:, :], preferred_element_type=F32)
        yn = y * lax.rsqrt(jnp.mean(y * y, axis=-1, keepdims=True) + EPS) * g_ref[...]
        o_ref[rows, :] = h_ref[rows, :] + mod_ref[2:3] * yn


def _out_projection(oa, ob, on, w_bf16, h2d, mods, mod_row_of_block, g_post, *, layer, tm):
    rows, d = h2d.shape
    kdim = w_bf16.shape[1]
    return pl.pallas_call(
        _outproj_kernel,
        grid=(rows // tm,),
        in_specs=[
            pl.BlockSpec((tm, oa.shape[1]), lambda i: (i, 0)),
            pl.BlockSpec((tm, ob.shape[1]), lambda i: (i, 0)),
            pl.BlockSpec((tm, on.shape[1]), lambda i: (i, 0)),
            pl.BlockSpec((None, kdim, d), lambda i: (layer, 0, 0), pipeline_mode=pl.Buffered(1)),
            pl.BlockSpec((tm, d), lambda i: (i, 0)),
            pl.BlockSpec((None, 6, d), lambda i: (mod_row_of_block(i), 0, 0)),
            pl.BlockSpec((1, d), lambda i: (0, 0)),
        ],
        out_specs=pl.BlockSpec((tm, d), lambda i: (i, 0)),
        out_shape=jax.ShapeDtypeStruct((rows, d), F32),
        compiler_params=_cparams(("parallel",)),
        name="out_projection",
    )(oa, ob, on, w_bf16, h2d, mods, g_post)


def _ffn_kernel(h_ref, mod_ref, gpre_ref, wg_ref, wu_ref, wd_ref, gpost_ref, o_ref, xm_ref):
    f = pl.program_id(1)
    last = pl.num_programs(1) - 1
    tm = o_ref.shape[0]
    slab = min(tm, 2 * HEAD_DIM)

    def swiglu_down(xm):
        gate = jnp.dot(xm, wg_ref[...], preferred_element_type=F32)
        up = jnp.dot(xm, wu_ref[...], preferred_element_type=F32)
        act = (gate * jax.nn.sigmoid(gate) * up).astype(BF16)
        return jnp.dot(act, wd_ref[...], preferred_element_type=F32)

    @pl.when(f == 0)
    def _():
        for r0 in range(0, tm, slab):
            rows = slice(r0, r0 + slab)
            ms = jnp.mean(jnp.square(h_ref[rows, :]), axis=-1, keepdims=True)
            xn = h_ref[rows, :] * lax.rsqrt(ms + EPS) * gpre_ref[...]
            xm = (xn * (1.0 + mod_ref[4:5]) + mod_ref[3:4]).astype(BF16)
            xm_ref[rows, :] = xm
            o_ref[rows, :] = swiglu_down(xm)

    @pl.when(jnp.logical_and(f > 0, f < last))
    def _():
        o_ref[...] += swiglu_down(xm_ref[...])

    @pl.when(f == last)
    def _():
        for r0 in range(0, tm, slab):
            rows = slice(r0, r0 + slab)
            o_ref[rows, :] += swiglu_down(xm_ref[rows, :])
            ms = jnp.mean(jnp.square(o_ref[rows, :]), axis=-1, keepdims=True)
            yn = o_ref[rows, :] * lax.rsqrt(ms + EPS) * gpost_ref[...]
            o_ref[rows, :] = h_ref[rows, :] + mod_ref[5:6] * yn


def _ffn(h2d, mods, mod_row_of_block, g_pre, wg, wu, wd, g_post, *, layer, tm, tf):
    rows, d = h2d.shape
    ff = wg.shape[2]
    assert ff // tf >= 2
    return pl.pallas_call(
        _ffn_kernel,
        grid=(rows // tm, ff // tf),
        in_specs=[
            pl.BlockSpec((tm, d), lambda i, f: (i, 0)),
            pl.BlockSpec((None, 6, d), lambda i, f: (mod_row_of_block(i), 0, 0)),
            pl.BlockSpec((1, d), lambda i, f: (0, 0)),
            pl.BlockSpec((None, d, tf), lambda i, f: (layer, 0, f)),
            pl.BlockSpec((None, d, tf), lambda i, f: (layer, 0, f)),
            pl.BlockSpec((None, tf, d), lambda i, f: (layer, f, 0)),
            pl.BlockSpec((1, d), lambda i, f: (0, 0)),
        ],
        out_specs=pl.BlockSpec((tm, d), lambda i, f: (i, 0)),
        out_shape=jax.ShapeDtypeStruct((rows, d), F32),
        scratch_shapes=[pltpu.VMEM((tm, d), BF16)],
        compiler_params=_cparams(("parallel", "arbitrary")),
        name="ffn",
    )(h2d, mods, g_pre, wg, wu, wd, g_post)


def kernel(x, c, ctx, c_ctx, w_mod, b_mod, g_pre1, g_post1, g_pre2, g_post2, w_in, w_out,
           lam_q1, lam_k1, lam_q2, lam_k2, g_diff, g_qn, g_kn, rpb, w_gate, w_up, w_down):
    batch, seq, d = x.shape
    ctx_len = ctx.shape[1]
    depth = w_mod.shape[0]
    n_heads = d // HEAD_DIM
    a_heads, b_heads, c_heads = n_heads // 4, n_heads // 2, n_heads // 4
    b_kv = b_heads // 4
    group = b_heads // b_kv
    a_qk = a_v = a_heads * HEAD_DIM
    b_q, b_kvw, c_w = b_heads * HEAD_DIM, b_kv * HEAD_DIM, c_heads * HEAD_DIM
    sizes = (a_qk, a_qk, a_v, b_q, b_kvw, b_kvw, c_w, c_w, c_w)
    starts = np.cumsum((0,) + sizes)[:-1] // HEAD_DIM
    _, ka_c, _, _, kb_c, _, qn_c, kn_c, vn_c = (int(s) for s in starts)
    qta_u, qtb_u = 0, a_heads
    vta_u, vtb_u = a_heads + b_heads, 2 * a_heads + b_heads
    assert batch + 1 <= MOD_ROWS and seq % GRID_W == 0

    tm = _pick(seq, (512, 256, 128))
    tm_c = _pick(ctx_len, (512, 256, 128))
    tq = _pick(seq, (256, 128))
    tq_c = _pick(ctx_len, (256, 128))
    tk = 512
    tf = _pick(w_gate.shape[2], (512, 256, 128))
    tm_ffn = _pick(seq, (1024, 512, 256, 128))
    tm_ffn_c = _pick(batch * ctx_len, (1024, 512, 256, 128))
    hps = 2 if a_heads % 2 == 0 else 1

    c_rows = jnp.zeros((MOD_ROWS, d), F32).at[:batch].set(c).at[batch].set(c_ctx)
    mods = _mod_vectors(c_rows, w_mod, b_mod).reshape(depth, MOD_ROWS, 6, d)
    tabs_a = _rope_tables(seq, HEAD_DIM // 2)
    tabs_b = _rope_tables(seq, HEAD_DIM)
    tabs_c = jnp.zeros((3, tm_c, HEAD_DIM), F32)

    lat_row = lambda i: i // (seq // tm)
    ctx_row = lambda i: batch
    row1 = lambda v: v.reshape(1, -1)

    w_in_b, w_out_b = _to_bf16(w_in), _to_bf16(w_out)
    wg, wu, wd = _to_bf16(w_gate), _to_bf16(w_up), _to_bf16(w_down)

    h = x.reshape(batch * seq, d)
    hc = ctx.reshape(batch * ctx_len, d)
    for l in range(depth):
        need_ctx = l < depth - 1
        lam_init = 0.8 - 0.6 * math.exp(-0.3 * l)
        lams = [row1(v[l]) for v in (lam_q1, lam_k1, lam_q2, lam_k2)]
        inproj = functools.partial(_in_projection, g_pre=row1(g_pre1[l]), w_bf16=w_in_b, layer=l,
                                   g_qn=row1(g_qn[l]), g_kn=row1(g_kn[l]), sizes=sizes, batch=batch)
        qkv, vt = inproj(h, mods[l], lat_row, tabs_a=tabs_a, tabs_b=tabs_b, seq=seq, tm=tm, use_rope=True)
        qkv_c, vt_c = inproj(hc, mods[l], ctx_row, tabs_a=tabs_c, tabs_b=tabs_c, seq=ctx_len, tm=tm_c,
                             use_rope=False)
        both = [(qkv_c, vt_c, ctx_len), (qkv, vt, seq)]
        only_ctx = [(qkv_c, vt_c, ctx_len)]

        oa = _diff_attention(vt, both, lams, row1(g_diff[l]), batch=batch, sq=seq, heads=a_heads,
                             qt_unit=qta_u, k_col=ka_c, vt_unit=vta_u, hps=hps, tq=tq, tk=tk,
                             lam_init=lam_init, name="diff_attention")
        ob = _gqa_attention(vt, qtb_u, both, kb_c, vtb_u, batch=batch, sq=seq, n_kv=b_kv, group=group,
                            tq=tq, tk=tk, name="gqa_attention")
        on = _na_attention(qkv, qkv_c, rpb[l], batch=batch, seq=seq, ctx_len=ctx_len, heads=c_heads,
                           q_col=qn_c, k_col=kn_c, v_col=vn_c, tq=tq)
        h_mid = _out_projection(oa, ob, on, w_out_b, h, mods[l], lat_row, row1(g_post1[l]), layer=l, tm=tm)
        h_new = _ffn(h_mid, mods[l], lambda i: i // (seq // tm_ffn), row1(g_pre2[l]), wg, wu, wd,
                     row1(g_post2[l]), layer=l, tm=tm_ffn, tf=tf)

        if need_ctx:
            oac = _diff_attention(vt_c, only_ctx, lams, row1(g_diff[l]), batch=batch, sq=ctx_len,
                                  heads=a_heads, qt_unit=qta_u, k_col=ka_c, vt_unit=vta_u, hps=hps,
                                  tq=tq_c, tk=tk, lam_init=lam_init, name="diff_attention_ctx")
            obc = _gqa_attention(vt_c, qtb_u, only_ctx, kb_c, vtb_u, batch=batch, sq=ctx_len, n_kv=b_kv,
                                 group=group, tq=tq_c, tk=tk, name="gqa_attention_ctx")
            onc = _dense_ctx_attention(qkv_c, batch=batch, ctx_len=ctx_len, heads=c_heads, q_col=qn_c,
                                       k_col=kn_c, v_col=vn_c)
            hc_mid = _out_projection(oac, obc, onc, w_out_b, hc, mods[l], ctx_row, row1(g_post1[l]),
                                     layer=l, tm=tm_c)
            hc = _ffn(hc_mid, mods[l], ctx_row, row1(g_pre2[l]), wg, wu, wd, row1(g_post2[l]),
                      layer=l, tm=tm_ffn_c, tf=tf)
        h = h_new
    return h.reshape(batch, seq, d)
```

```python
import functools
import math

import numpy as np
import jax
import jax.numpy as jnp
from jax import lax
from jax.experimental import pallas as pl
from jax.experimental.pallas import tpu as pltpu

GRID_W = 64
HEAD_DIM = 128
NA_COLS = 16
ROPE_THETA = 10000.0
EPS = 1e-6
NEG_BIG = -1e30
LOG2E = math.log2(math.e)
MOD_ROWS = 8
V7X_VMEM_LIMIT = 56 * 1024 * 1024

F32 = jnp.float32
BF16 = jnp.bfloat16


def _cparams(sem):
    return pltpu.CompilerParams(dimension_semantics=sem, vmem_limit_bytes=V7X_VMEM_LIMIT)


def _pick(n, candidates):
    for c in candidates:
        if n % c == 0:
            return c
    raise ValueError(f"no tile in {candidates} divides {n}")


def _mod_kernel(c_ref, w_ref, b_ref, o_ref):
    c = c_ref[...]
    a = (c * jax.nn.sigmoid(c)).astype(BF16)
    o_ref[0] = jnp.dot(a, w_ref[0].astype(BF16), preferred_element_type=F32) + b_ref[0]


def _mod_vectors(c_rows, w_mod, b_mod):
    depth, d, n = w_mod.shape
    tn = _pick(n, (1024, 512, 256, 128))
    return pl.pallas_call(
        _mod_kernel,
        grid=(depth, n // tn),
        in_specs=[
            pl.BlockSpec((MOD_ROWS, d), lambda l, j: (0, 0)),
            pl.BlockSpec((1, d, tn), lambda l, j: (l, 0, j)),
            pl.BlockSpec((1, 1, tn), lambda l, j: (l, 0, j)),
        ],
        out_specs=pl.BlockSpec((1, MOD_ROWS, tn), lambda l, j: (l, 0, j)),
        out_shape=jax.ShapeDtypeStruct((depth, MOD_ROWS, n), F32),
        compiler_params=_cparams(("parallel", "parallel")),
        name="mod_vectors",
    )(c_rows, w_mod, b_mod.reshape(depth, 1, n))


def _cast_kernel(w_ref, o_ref):
    o_ref[...] = w_ref[...].astype(BF16)


def _to_bf16(w):
    depth, rows, cols = w.shape
    tr = _pick(rows, (256, 128))
    return pl.pallas_call(
        _cast_kernel,
        grid=(depth, rows // tr),
        in_specs=[pl.BlockSpec((None, tr, cols), lambda l, i: (l, i, 0))],
        out_specs=pl.BlockSpec((None, tr, cols), lambda l, i: (l, i, 0)),
        out_shape=jax.ShapeDtypeStruct(w.shape, BF16),
        compiler_params=_cparams(("parallel", "parallel")),
        name="weights_to_bf16",
    )(w)


def _rope_tables(seq, head_width):
    quarter = head_width // 4
    lane = np.arange(HEAD_DIM)
    within_head = lane % head_width
    use_col = (within_head // (2 * quarter)) == 1
    within_half = within_head % (2 * quarter)
    first = within_half < quarter
    freq = within_half % quarter
    t = jnp.arange(seq)
    row, col = t // GRID_W, t % GRID_W
    inv = (ROPE_THETA ** (-jnp.arange(quarter, dtype=F32) / quarter))[freq]
    pos = jnp.where(use_col[None, :], col.astype(F32)[:, None], row.astype(F32)[:, None])
    ang = pos * inv[None, :]
    cos, sin = jnp.cos(ang), jnp.sin(ang)
    zero = jnp.zeros_like(sin)
    return jnp.stack([cos, jnp.where(first[None, :], -sin, zero), jnp.where(first[None, :], zero, sin)])


def _rope(y, tab_ref, quarter):
    up = pltpu.roll(y, HEAD_DIM - quarter, 1)
    down = pltpu.roll(y, quarter, 1)
    return y * tab_ref[0] + up * tab_ref[1] + down * tab_ref[2]


def _col_plan(sizes):
    a_qk, _, a_v, b_q, b_kv, _, c_w, _, _ = sizes
    u = lambda width: width // HEAD_DIM
    s_a = (HEAD_DIM // 2) ** -0.5 * LOG2E
    s_b = HEAD_DIM ** -0.5 * LOG2E
    t_qb, t_va, t_vb = u(a_qk), u(a_qk) + u(b_q), u(a_qk) + u(b_q) + u(a_v)
    plan = ([("a", None, s_a, i) for i in range(u(a_qk))] + [("a", None, None, None)] * u(a_qk)
            + [(None, None, None, t_va + i) for i in range(u(a_v))]
            + [("b", "q", s_b, t_qb + i) for i in range(u(b_q))] + [("b", "k", None, None)] * u(b_kv)
            + [(None, None, None, t_vb + i) for i in range(u(b_kv))]
            + [(None, None, s_b, None)] * u(c_w) + [(None, None, None, None)] * (2 * u(c_w)))
    return plan


def _inproj_kernel(x_ref, mod_ref, g_ref, w_ref, gq_ref, gk_ref, ta_ref, tb_ref, o_ref, vt_ref, *,
                   plan, chunk_units, use_rope):
    x = x_ref[...]
    m = mod_ref[...]
    ms = jnp.mean(x * x, axis=-1, keepdims=True)
    xn = x * lax.rsqrt(ms + EPS) * g_ref[...]
    xm = (xn * (1.0 + m[1:2]) + m[0:1]).astype(BF16)
    n_units = len(plan)
    for u0 in range(0, n_units, chunk_units):
        u1 = min(u0 + chunk_units, n_units)
        y = jnp.dot(xm, w_ref[:, u0 * HEAD_DIM:u1 * HEAD_DIM], preferred_element_type=F32)
        for u in range(u0, u1):
            yu = y[:, (u - u0) * HEAD_DIM:(u - u0 + 1) * HEAD_DIM]
            rope, norm, q_scale, vt_slot = plan[u]
            if norm is not None:
                gain = gq_ref[...] if norm == "q" else gk_ref[...]
                yu = yu * lax.rsqrt(jnp.mean(yu * yu, axis=-1, keepdims=True) + EPS) * gain
            if rope is not None and use_rope:
                yu = _rope(yu, ta_ref, HEAD_DIM // 8) if rope == "a" else _rope(yu, tb_ref, HEAD_DIM // 4)
            if q_scale is not None:
                yu = yu * q_scale
            o_ref[:, u * HEAD_DIM:(u + 1) * HEAD_DIM] = yu.astype(BF16)
            if vt_slot is not None:
                vt_ref[vt_slot * HEAD_DIM:(vt_slot + 1) * HEAD_DIM, :] = yu.T.astype(BF16)


def _in_projection(h2d, mods, mod_row_of_block, g_pre, w_bf16, g_qn, g_kn, tabs_a, tabs_b, *,
                   layer, sizes, batch, seq, tm, use_rope):
    rows, d = h2d.shape
    n = w_bf16.shape[2]
    plan = _col_plan(sizes)
    n_vt = sum(1 for p in plan if p[3] is not None)
    bps = seq // tm
    kern = functools.partial(_inproj_kernel, plan=plan, chunk_units=4, use_rope=use_rope)
    return pl.pallas_call(
        kern,
        grid=(rows // tm,),
        in_specs=[
            pl.BlockSpec((tm, d), lambda i: (i, 0)),
            pl.BlockSpec((None, 6, d), lambda i: (mod_row_of_block(i), 0, 0)),
            pl.BlockSpec((1, d), lambda i: (0, 0)),
            pl.BlockSpec((None, d, n), lambda i: (layer, 0, 0), pipeline_mode=pl.Buffered(1)),
            pl.BlockSpec((1, HEAD_DIM), lambda i: (0, 0)),
            pl.BlockSpec((1, HEAD_DIM), lambda i: (0, 0)),
            pl.BlockSpec((3, tm, HEAD_DIM), lambda i: (0, i % bps, 0)),
            pl.BlockSpec((3, tm, HEAD_DIM), lambda i: (0, i % bps, 0)),
        ],
        out_specs=[pl.BlockSpec((tm, n), lambda i: (i, 0)),
                   pl.BlockSpec((None, n_vt * HEAD_DIM, tm), lambda i: (i // bps, 0, i % bps))],
        out_shape=[jax.ShapeDtypeStruct((rows, n), BF16),
                   jax.ShapeDtypeStruct((batch, n_vt * HEAD_DIM, seq), BF16)],
        compiler_params=_cparams(("parallel",)),
        name="in_projection",
    )(h2d, mods, g_pre, w_bf16, g_qn, g_kn, tabs_a, tabs_b)


def _fold_rows(x, op):
    slabs = [x[r:r + 8] for r in range(0, x.shape[0], 8)]
    while len(slabs) > 1:
        nxt = [op(slabs[i], slabs[i + 1]) for i in range(0, len(slabs) - 1, 2)]
        if len(slabs) % 2:
            nxt.append(slabs[-1])
        slabs = nxt
    return slabs[0]


def _scores(q_list, k):
    out = []
    for qt, u, _ in q_list:
        s = jnp.dot(k[:, u * HEAD_DIM:(u + 1) * HEAD_DIM], qt, preferred_element_type=F32)
        out.append((s, _fold_rows(s, jnp.maximum)))
    return out


def _softmax_pv(stats, scores, q_list, vt, acc_ref):
    new = []
    for c, ((m, l), (s, smax), (_, _, v_unit)) in enumerate(zip(stats, scores, q_list)):
        m_new = jnp.maximum(m, jnp.max(smax, axis=0, keepdims=True))
        alpha = jnp.exp2(m - m_new)
        p = jnp.exp2(s - m_new)
        l = alpha * l + jnp.sum(_fold_rows(p, jnp.add), axis=0, keepdims=True)
        vv = vt[v_unit * HEAD_DIM:(v_unit + 1) * HEAD_DIM, :]
        acc_ref[c] = alpha * acc_ref[c] + jnp.dot(vv, p.astype(BF16), preferred_element_type=F32)
        new.append((m_new, l))
    return tuple(new)


def _attn_chains(q_list, sources, tq, tk, s_refs, mx_refs, acc_ref):
    n_ch = len(q_list)
    acc_ref[...] = jnp.zeros_like(acc_ref)
    stats = tuple((jnp.full((1, tq), NEG_BIG, F32), jnp.zeros((1, tq), F32)) for _ in q_list)
    kc_ref, vtc_ref = sources[0]
    if len(sources) == 2:
        k_ref, vt_ref = sources[1]
        n = k_ref.shape[0] // tk
        assert n >= 2 and n % 2 == 0 and n * tk == k_ref.shape[0]

        def keys_of(c):
            return pl.ds(c * tk if isinstance(c, int) else pl.multiple_of(c * tk, tk), tk)

        def put(slot, c):
            for i, (s, smax) in enumerate(_scores(q_list, k_ref[keys_of(c), :])):
                s_refs[slot][i] = s
                mx_refs[slot][i] = smax

        def take(st, slot, c):
            return _softmax_pv(st, [(s_refs[slot][i], mx_refs[slot][i]) for i in range(n_ch)], q_list,
                               vt_ref[:, keys_of(c)], acc_ref)

        put(0, 0)

        def body(j, st):
            a = 2 * j
            put(1, a + 1)
            st = take(st, 0, a)
            put(0, a + 2)
            return take(st, 1, a + 1)

        stats = lax.fori_loop(0, n // 2 - 1, body, stats)
        put(1, n - 1)
        stats = take(stats, 0, n - 2)
        s_ctx = _scores(q_list, kc_ref[...])
        stats = take(stats, 1, n - 1)
    else:
        s_ctx = _scores(q_list, kc_ref[...])
    stats = _softmax_pv(stats, s_ctx, q_list, vtc_ref[...], acc_ref)
    return [(acc_ref[c] / l).T for c, (_, l) in enumerate(stats)]


def _attn_scratch(n_chains, tq, tk):
    scores = pltpu.VMEM((n_chains, tk, tq), F32)
    maxima = pltpu.VMEM((n_chains, 8, tq), F32)
    return [scores, scores, maxima, maxima, pltpu.VMEM((n_chains, HEAD_DIM, tq), F32)]


def _gqa_kernel(*refs, group, tk, n_sources):
    qt_ref = refs[0]
    kv = refs[1:1 + 2 * n_sources]
    o_ref, s0_ref, s1_ref, mx0_ref, mx1_ref, acc_ref = refs[1 + 2 * n_sources:]
    sources = [(kv[2 * s], kv[2 * s + 1]) for s in range(n_sources)]
    tq = qt_ref.shape[1]
    q_list = [(qt_ref[g * HEAD_DIM:(g + 1) * HEAD_DIM, :], 0, 0) for g in range(group)]
    outs = _attn_chains(q_list, sources, tq, tk, (s0_ref, s1_ref), (mx0_ref, mx1_ref), acc_ref)
    for g, o in enumerate(outs):
        o_ref[:, g * HEAD_DIM:(g + 1) * HEAD_DIM] = o.astype(BF16)


def _gqa_attention(qt_arr, qt_unit, kv_arrs, k_col, vt_unit, *, batch, sq, n_kv, group, tq, tk, name):
    nq = sq // tq
    gw = group * HEAD_DIM
    assert qt_unit % group == 0
    in_specs = [pl.BlockSpec((None, gw, tq), lambda b, h, i: (b, qt_unit // group + h, i))]
    args = [qt_arr]
    for k_arr, vt_arr, t in kv_arrs:
        in_specs.append(pl.BlockSpec((t, HEAD_DIM), lambda b, h, i: (b, k_col + h)))
        in_specs.append(pl.BlockSpec((None, HEAD_DIM, t), lambda b, h, i: (b, vt_unit + h, 0)))
        args += [k_arr, vt_arr]
    kern = functools.partial(_gqa_kernel, group=group, tk=tk, n_sources=len(kv_arrs))
    return pl.pallas_call(
        kern,
        grid=(batch, n_kv, nq),
        in_specs=in_specs,
        out_specs=pl.BlockSpec((tq, gw), lambda b, h, i: (b * nq + i, h)),
        out_shape=jax.ShapeDtypeStruct((batch * sq, n_kv * gw), BF16),
        scratch_shapes=_attn_scratch(group, tq, tk),
        compiler_params=_cparams(("parallel", "parallel", "parallel")),
        name=name,
    )(*args)


def _diff_kernel(*refs, hps, tk, lam_init, n_sources):
    qt_ref, lq1, lk1, lq2, lk2, g_ref = refs[:6]
    kv = refs[6:6 + 2 * n_sources]
    o_ref, s0_ref, s1_ref, mx0_ref, mx1_ref, acc_ref = refs[6 + 2 * n_sources:]
    sources = [(kv[2 * s], kv[2 * s + 1]) for s in range(n_sources)]
    tq = qt_ref.shape[1]
    half = HEAD_DIM // 2
    lam = (jnp.exp(jnp.sum(lq1[...] * lk1[...], axis=-1, keepdims=True))
           - jnp.exp(jnp.sum(lq2[...] * lk2[...], axis=-1, keepdims=True)) + lam_init)
    dim = lax.broadcasted_iota(jnp.int32, (HEAD_DIM, tq), 0)
    q_list = []
    for h in range(hps):
        qt = qt_ref[h * HEAD_DIM:(h + 1) * HEAD_DIM, :]
        zero = jnp.zeros_like(qt)
        q_list.append((jnp.where(dim < half, qt, zero), h, h))
        q_list.append((jnp.where(dim < half, zero, qt), h, h))
    outs = _attn_chains(q_list, sources, tq, tk, (s0_ref, s1_ref), (mx0_ref, mx1_ref), acc_ref)
    for h in range(hps):
        o = outs[2 * h] - lam * outs[2 * h + 1]
        o = o * lax.rsqrt(jnp.mean(o * o, axis=-1, keepdims=True) + EPS) * g_ref[...]
        o_ref[:, h * HEAD_DIM:(h + 1) * HEAD_DIM] = (o * (1.0 - lam_init)).astype(BF16)


def _diff_attention(qt_arr, kv_arrs, lams, g_diff, *, batch, sq, heads, qt_unit, k_col, vt_unit,
                    hps, tq, tk, lam_init, name):
    nq = sq // tq
    half = HEAD_DIM // 2
    w = hps * HEAD_DIM
    assert heads % hps == 0 and qt_unit % hps == 0 and k_col % hps == 0 and vt_unit % hps == 0
    small = lambda width: pl.BlockSpec((1, width), lambda b, h, i: (0, 0))
    in_specs = [pl.BlockSpec((None, w, tq), lambda b, h, i: (b, qt_unit // hps + h, i)),
                small(half), small(half), small(half), small(half), small(HEAD_DIM)]
    args = [qt_arr, *lams, g_diff]
    for k_arr, vt_arr, t in kv_arrs:
        in_specs.append(pl.BlockSpec((t, w), lambda b, h, i: (b, k_col // hps + h)))
        in_specs.append(pl.BlockSpec((None, w, t), lambda b, h, i: (b, vt_unit // hps + h, 0)))
        args += [k_arr, vt_arr]
    kern = functools.partial(_diff_kernel, hps=hps, tk=tk, lam_init=lam_init, n_sources=len(kv_arrs))
    return pl.pallas_call(
        kern,
        grid=(batch, heads // hps, nq),
        in_specs=in_specs,
        out_specs=pl.BlockSpec((tq, w), lambda b, h, i: (b * nq + i, h)),
        out_shape=jax.ShapeDtypeStruct((batch * sq, heads * HEAD_DIM), BF16),
        scratch_shapes=_attn_scratch(2 * hps, tq, tk),
        compiler_params=_cparams(("parallel", "parallel", "parallel")),
        name=name,
    )(*args)


def _na_plan(seq, tq, wr_max, win_rows):
    rows_n = seq // GRID_W
    wr = min(wr_max, rows_n)
    r_per = tq // GRID_W
    outside = 2 * wr_max - 1
    starts, pats = [], []
    for blk in range(seq // tq):
        rf = blk * r_per
        lo = int(np.clip(rf - wr // 2, 0, rows_n - wr))
        w0 = min(lo, rows_n - win_rows)
        r = rf + np.arange(r_per)
        kr = w0 + np.arange(win_rows)
        r0 = np.clip(r - wr // 2, 0, rows_n - wr)
        row_ok = (kr[None, :] >= r0[:, None]) & (kr[None, :] < r0[:, None] + wr)
        roff = kr[None, :] - r[:, None] + wr_max - 1
        starts.append(w0 * GRID_W)
        pats.append(np.where(row_ok, roff, outside))
    uniq, pids = [], []
    for p in pats:
        for j, u in enumerate(uniq):
            if np.array_equal(u, p):
                pids.append(j)
                break
        else:
            pids.append(len(uniq))
            uniq.append(p)
    return np.asarray(starts, np.int32), np.asarray(pids, np.int32), np.stack(uniq)


def _na_bias(rpb_l, row_pats, tq, wk):
    heads = rpb_l.shape[0]
    cidx = np.arange(GRID_W)
    c0 = np.clip(cidx - NA_COLS // 2, 0, GRID_W - NA_COLS)
    col_ok = (cidx[None, :] >= c0[:, None]) & (cidx[None, :] < c0[:, None] + NA_COLS)
    coff = np.clip(cidx[None, :] - cidx[:, None], -(NA_COLS - 1), NA_COLS - 1) + NA_COLS - 1
    n_coff = rpb_l.shape[2]
    onehot = jnp.asarray(coff.reshape(-1)[None, :] == np.arange(n_coff)[:, None], F32)
    picked = jnp.dot(rpb_l.astype(F32).reshape(-1, n_coff), onehot, precision=lax.Precision.HIGHEST)
    picked = picked.reshape(heads, rpb_l.shape[1], GRID_W, GRID_W)
    by_col = jnp.where(col_ok[None, None], picked * LOG2E, NEG_BIG)
    outside = jnp.full((heads, GRID_W, GRID_W), NEG_BIG, F32)
    n_rel = by_col.shape[1]
    tile = lambda rel: by_col[:, rel] if rel < n_rel else outside
    strips = [jnp.concatenate([tile(int(rel)) for rel in q_row], axis=-1)
              for pat in row_pats for q_row in pat]
    n_pat = row_pats.shape[0]
    return jnp.stack(strips, axis=1).reshape(heads, n_pat, tq, wk)


def _na_kernel(w0_ref, pid_ref, q_ref, kc_ref, vc_ref, k_ref, v_ref, bias_ref, o_ref, *, heads, wk):
    del pid_ref
    w0 = pl.multiple_of(w0_ref[pl.program_id(1)], GRID_W)
    win = pl.ds(w0, wk)
    nt = (((1,), (1,)), ((), ()))
    head_lanes = [slice(h * HEAD_DIM, (h + 1) * HEAD_DIM) for h in range(heads)]
    scores = []
    for h, lanes in enumerate(head_lanes):
        q = q_ref[:, lanes]
        s_w = lax.dot_general(q, k_ref[win, lanes], nt, preferred_element_type=F32) + bias_ref[h]
        s_c = lax.dot_general(q, kc_ref[:, lanes], nt, preferred_element_type=F32)
        scores.append((s_w, s_c))
    probs = []
    for s_w, s_c in scores:
        m = jnp.maximum(jnp.max(s_w, axis=-1, keepdims=True), jnp.max(s_c, axis=-1, keepdims=True))
        p_w = jnp.exp2(s_w - m)
        p_c = jnp.exp2(s_c - m)
        l = jnp.sum(p_w, axis=-1, keepdims=True) + jnp.sum(p_c, axis=-1, keepdims=True)
        probs.append((p_w.astype(BF16), p_c.astype(BF16), l))
    for (p_w, p_c, l), lanes in zip(probs, head_lanes):
        acc = (jnp.dot(p_w, v_ref[win, lanes], preferred_element_type=F32)
               + jnp.dot(p_c, vc_ref[:, lanes], preferred_element_type=F32))
        o_ref[:, lanes] = (acc / l).astype(BF16)


def _na_attention(qkv, qkv_c, rpb_l, *, batch, seq, ctx_len, heads, q_col, k_col, v_col, tq):
    wr_max = (rpb_l.shape[1] + 1) // 2
    r_per = tq // GRID_W
    win_rows = min(-(-(r_per + wr_max - 1) // 4) * 4, seq // GRID_W)
    wk = win_rows * GRID_W
    starts, pids, row_pats = _na_plan(seq, tq, wr_max, win_rows)
    bias = _na_bias(rpb_l, row_pats, tq, wk)
    nq = seq // tq
    w = heads * HEAD_DIM
    kern = functools.partial(_na_kernel, heads=heads, wk=wk)
    grid_spec = pltpu.PrefetchScalarGridSpec(
        num_scalar_prefetch=2,
        grid=(batch, nq),
        in_specs=[
            pl.BlockSpec((tq, w), lambda b, i, w0, pid: (b * nq + i, q_col // heads)),
            pl.BlockSpec((ctx_len, w), lambda b, i, w0, pid: (b, k_col // heads)),
            pl.BlockSpec((ctx_len, w), lambda b, i, w0, pid: (b, v_col // heads)),
            pl.BlockSpec((seq, w), lambda b, i, w0, pid: (b, k_col // heads)),
            pl.BlockSpec((seq, w), lambda b, i, w0, pid: (b, v_col // heads)),
            pl.BlockSpec((heads, None, tq, wk), lambda b, i, w0, pid: (0, pid[i], 0, 0)),
        ],
        out_specs=pl.BlockSpec((tq, w), lambda b, i, w0, pid: (b * nq + i, 0)),
    )
    return pl.pallas_call(
        kern,
        grid_spec=grid_spec,
        out_shape=jax.ShapeDtypeStruct((batch * seq, w), BF16),
        compiler_params=_cparams(("parallel", "parallel")),
        name="na_attention",
    )(jnp.asarray(starts), jnp.asarray(pids), qkv, qkv_c, qkv_c, qkv, qkv, bias)


def _dense_ctx_kernel(q_ref, k_ref, v_ref, o_ref, *, heads):
    for h in range(heads):
        lanes = slice(h * HEAD_DIM, (h + 1) * HEAD_DIM)
        s = lax.dot_general(q_ref[:, lanes], k_ref[:, lanes], (((1,), (1,)), ((), ())),
                            preferred_element_type=F32)
        p = jnp.exp2(s - jnp.max(s, axis=-1, keepdims=True))
        acc = jnp.dot(p.astype(BF16), v_ref[:, lanes], preferred_element_type=F32)
        o_ref[:, lanes] = (acc / jnp.sum(p, axis=-1, keepdims=True)).astype(BF16)


def _dense_ctx_attention(qkv_c, *, batch, ctx_len, heads, q_col, k_col, v_col):
    w = heads * HEAD_DIM
    spec = lambda col: pl.BlockSpec((ctx_len, w), lambda b: (b, col // heads))
    return pl.pallas_call(
        functools.partial(_dense_ctx_kernel, heads=heads),
        grid=(batch,),
        in_specs=[spec(q_col), spec(k_col), spec(v_col)],
        out_specs=pl.BlockSpec((ctx_len, w), lambda b: (b, 0)),
        out_shape=jax.ShapeDtypeStruct((batch * ctx_len, w), BF16),
        compiler_params=_cparams(("parallel",)),
        name="dense_attention_ctx",
    )(qkv_c, qkv_c, qkv_c)


def _outproj_kernel(oa_ref, ob_ref, on_ref, w_ref, h_ref, mod_ref, g_ref, o_ref):
    wa, wb = oa_ref.shape[1], ob_ref.shape[1]
    tm = o_ref.shape[0]
    slab = min(tm, 2 * HEAD_DIM)
    for r0 in range(0, tm, slab):
        rows = slice(r0, r0 + slab)
        y = jnp.dot(oa_ref[rows, :], w_ref[:wa, :], preferred_element_type=F32)
        y += jnp.dot(ob_ref[rows, :], w_ref[wa:wa + wb, :], preferred_element_type=F32)
        y += jnp.dot(on_ref[rows, :], w_ref[wa + wb:, :], preferred_element_type=F32)
        yn = y * lax.rsqrt(jnp.mean(y * y, axis=-1, keepdims=True) + EPS) * g_ref[...]
        o_ref[rows, :] = h_ref[rows, :] + mod_ref[2:3] * yn


def _out_projection(oa, ob, on, w_bf16, h2d, mods, mod_row_of_block, g_post, *, layer, tm):
    rows, d = h2d.shape
    kdim = w_bf16.shape[1]
    return pl.pallas_call(
        _outproj_kernel,
        grid=(rows // tm,),
        in_specs=[
            pl.BlockSpec((tm, oa.shape[1]), lambda i: (i, 0)),
            pl.BlockSpec((tm, ob.shape[1]), lambda i: (i, 0)),
            pl.BlockSpec((tm, on.shape[1]), lambda i: (i, 0)),
            pl.BlockSpec((None, kdim, d), lambda i: (layer, 0, 0), pipeline_mode=pl.Buffered(1)),
            pl.BlockSpec((tm, d), lambda i: (i, 0)),
            pl.BlockSpec((None, 6, d), lambda i: (mod_row_of_block(i), 0, 0)),
            pl.BlockSpec((1, d), lambda i: (0, 0)),
        ],
        out_specs=pl.BlockSpec((tm, d), lambda i: (i, 0)),
        out_shape=jax.ShapeDtypeStruct((rows, d), F32),
        compiler_params=_cparams(("parallel",)),
        name="out_projection",
    )(oa, ob, on, w_bf16, h2d, mods, g_post)


def _ffn_kernel(h_ref, mod_ref, gpre_ref, wg_ref, wu_ref, wd_ref, gpost_ref, o_ref, xm_ref):
    f = pl.program_id(1)
    last = pl.num_programs(1) - 1
    tm = o_ref.shape[0]
    slab = min(tm, 2 * HEAD_DIM)

    def swiglu_down(xm):
        gate = jnp.dot(xm, wg_ref[...], preferred_element_type=F32)
        up = jnp.dot(xm, wu_ref[...], preferred_element_type=F32)
        act = (gate * jax.nn.sigmoid(gate) * up).astype(BF16)
        return jnp.dot(act, wd_ref[...], preferred_element_type=F32)

    @pl.when(f == 0)
    def _():
        for r0 in range(0, tm, slab):
            rows = slice(r0, r0 + slab)
            ms = jnp.mean(jnp.square(h_ref[rows, :]), axis=-1, keepdims=True)
            xn = h_ref[rows, :] * lax.rsqrt(ms + EPS) * gpre_ref[...]
            xm = (xn * (1.0 + mod_ref[4:5]) + mod_ref[3:4]).astype(BF16)
            xm_ref[rows, :] = xm
            o_ref[rows, :] = swiglu_down(xm)

    @pl.when(jnp.logical_and(f > 0, f < last))
    def _():
        o_ref[...] += swiglu_down(xm_ref[...])

    @pl.when(f == last)
    def _():
        for r0 in range(0, tm, slab):
            rows = slice(r0, r0 + slab)
            o_ref[rows, :] += swiglu_down(xm_ref[rows, :])
            ms = jnp.mean(jnp.square(o_ref[rows, :]), axis=-1, keepdims=True)
            yn = o_ref[rows, :] * lax.rsqrt(ms + EPS) * gpost_ref[...]
            o_ref[rows, :] = h_ref[rows, :] + mod_ref[5:6] * yn


def _ffn(h2d, mods, mod_row_of_block, g_pre, wg, wu, wd, g_post, *, layer, tm, tf):
    rows, d = h2d.shape
    ff = wg.shape[2]
    assert ff // tf >= 2
    return pl.pallas_call(
        _ffn_kernel,
        grid=(rows // tm, ff // tf),
        in_specs=[
            pl.BlockSpec((tm, d), lambda i, f: (i, 0)),
            pl.BlockSpec((None, 6, d), lambda i, f: (mod_row_of_block(i), 0, 0)),
            pl.BlockSpec((1, d), lambda i, f: (0, 0)),
            pl.BlockSpec((None, d, tf), lambda i, f: (layer, 0, f)),
            pl.BlockSpec((None, d, tf), lambda i, f: (layer, 0, f)),
            pl.BlockSpec((None, tf, d), lambda i, f: (layer, f, 0)),
            pl.BlockSpec((1, d), lambda i, f: (0, 0)),
        ],
        out_specs=pl.BlockSpec((tm, d), lambda i, f: (i, 0)),
        out_shape=jax.ShapeDtypeStruct((rows, d), F32),
        scratch_shapes=[pltpu.VMEM((tm, d), BF16)],
        compiler_params=_cparams(("parallel", "arbitrary")),
        name="ffn",
    )(h2d, mods, g_pre, wg, wu, wd, g_post)


def kernel(x, c, ctx, c_ctx, w_mod, b_mod, g_pre1, g_post1, g_pre2, g_post2, w_in, w_out,
           lam_q1, lam_k1, lam_q2, lam_k2, g_diff, g_qn, g_kn, rpb, w_gate, w_up, w_down):
    batch, seq, d = x.shape
    ctx_len = ctx.shape[1]
    depth = w_mod.shape[0]
    n_heads = d // HEAD_DIM
    a_heads, b_heads, c_heads = n_heads // 4, n_heads // 2, n_heads // 4
    b_kv = b_heads // 4
    group = b_heads // b_kv
    a_qk = a_v = a_heads * HEAD_DIM
    b_q, b_kvw, c_w = b_heads * HEAD_DIM, b_kv * HEAD_DIM, c_heads * HEAD_DIM
    sizes = (a_qk, a_qk, a_v, b_q, b_kvw, b_kvw, c_w, c_w, c_w)
    starts = np.cumsum((0,) + sizes)[:-1] // HEAD_DIM
    _, ka_c, _, _, kb_c, _, qn_c, kn_c, vn_c = (int(s) for s in starts)
    qta_u, qtb_u = 0, a_heads
    vta_u, vtb_u = a_heads + b_heads, 2 * a_heads + b_heads
    assert batch + 1 <= MOD_ROWS and seq % GRID_W == 0

    tm = _pick(seq, (512, 256, 128))
    tm_c = _pick(ctx_len, (512, 256, 128))
    tq = _pick(seq, (256, 128))
    tq_dense = _pick(seq, (1024, 512, 256, 128))
    tq_c = _pick(ctx_len, (256, 128))
    tk = 512
    tf = _pick(w_gate.shape[2], (512, 256, 128))
    tm_ffn = _pick(seq, (1024, 512, 256, 128))
    tm_ffn_c = _pick(batch * ctx_len, (1024, 512, 256, 128))
    hps = 2 if a_heads % 2 == 0 else 1

    c_rows = jnp.zeros((MOD_ROWS, d), F32).at[:batch].set(c).at[batch].set(c_ctx)
    mods = _mod_vectors(c_rows, w_mod, b_mod).reshape(depth, MOD_ROWS, 6, d)
    tabs_a = _rope_tables(seq, HEAD_DIM // 2)
    tabs_b = _rope_tables(seq, HEAD_DIM)
    tabs_c = jnp.zeros((3, tm_c, HEAD_DIM), F32)

    lat_row = lambda i: i // (seq // tm)
    ctx_row = lambda i: batch
    row1 = lambda v: v.reshape(1, -1)

    w_in_b, w_out_b = _to_bf16(w_in), _to_bf16(w_out)
    wg, wu, wd = _to_bf16(w_gate), _to_bf16(w_up), _to_bf16(w_down)

    h = x.reshape(batch * seq, d)
    hc = ctx.reshape(batch * ctx_len, d)
    for l in range(depth):
        need_ctx = l < depth - 1
        lam_init = 0.8 - 0.6 * math.exp(-0.3 * l)
        lams = [row1(v[l]) for v in (lam_q1, lam_k1, lam_q2, lam_k2)]
        inproj = functools.partial(_in_projection, g_pre=row1(g_pre1[l]), w_bf16=w_in_b, layer=l,
                                   g_qn=row1(g_qn[l]), g_kn=row1(g_kn[l]), sizes=sizes, batch=batch)
        qkv, vt = inproj(h, mods[l], lat_row, tabs_a=tabs_a, tabs_b=tabs_b, seq=seq, tm=tm, use_rope=True)
        qkv_c, vt_c = inproj(hc, mods[l], ctx_row, tabs_a=tabs_c, tabs_b=tabs_c, seq=ctx_len, tm=tm_c,
                             use_rope=False)
        both = [(qkv_c, vt_c, ctx_len), (qkv, vt, seq)]
        only_ctx = [(qkv_c, vt_c, ctx_len)]

        oa = _diff_attention(vt, both, lams, row1(g_diff[l]), batch=batch, sq=seq, heads=a_heads,
                             qt_unit=qta_u, k_col=ka_c, vt_unit=vta_u, hps=hps, tq=tq_dense, tk=tk,
                             lam_init=lam_init, name="diff_attention")
        ob = _gqa_attention(vt, qtb_u, both, kb_c, vtb_u, batch=batch, sq=seq, n_kv=b_kv, group=group,
                            tq=tq_dense, tk=tk, name="gqa_attention")
        on = _na_attention(qkv, qkv_c, rpb[l], batch=batch, seq=seq, ctx_len=ctx_len, heads=c_heads,
                           q_col=qn_c, k_col=kn_c, v_col=vn_c, tq=tq)
        h_mid = _out_projection(oa, ob, on, w_out_b, h, mods[l], lat_row, row1(g_post1[l]), layer=l, tm=tm)
        h_new = _ffn(h_mid, mods[l], lambda i: i // (seq // tm_ffn), row1(g_pre2[l]), wg, wu, wd,
                     row1(g_post2[l]), layer=l, tm=tm_ffn, tf=tf)

        if need_ctx:
            oac = _diff_attention(vt_c, only_ctx, lams, row1(g_diff[l]), batch=batch, sq=ctx_len,
                                  heads=a_heads, qt_unit=qta_u, k_col=ka_c, vt_unit=vta_u, hps=hps,
                                  tq=tq_c, tk=tk, lam_init=lam_init, name="diff_attention_ctx")
            obc = _gqa_attention(vt_c, qtb_u, only_ctx, kb_c, vtb_u, batch=batch, sq=ctx_len, n_kv=b_kv,
                                 group=group, tq=tq_c, tk=tk, name="gqa_attention_ctx")
            onc = _dense_ctx_attention(qkv_c, batch=batch, ctx_len=ctx_len, heads=c_heads, q_col=qn_c,
                                       k_col=kn_c, v_col=vn_c)
            hc_mid = _out_projection(oac, obc, onc, w_out_b, hc, mods[l], ctx_row, row1(g_post1[l]),
                                     layer=l, tm=tm_c)
            hc = _ffn(hc_mid, mods[l], ctx_row, row1(g_pre2[l]), wg, wu, wd, row1(g_post2[l]),
                      layer=l, tm=tm_ffn_c, tf=tf)
        h = h_new
    return h.reshape(batch, seq, d)
```

```python
import functools
import math

import numpy as np
import jax
import jax.numpy as jnp
from jax import lax
from jax.experimental import pallas as pl
from jax.experimental.pallas import tpu as pltpu

GRID_W = 64
HEAD_DIM = 128
NA_COLS = 16
ROPE_THETA = 10000.0
EPS = 1e-6
NEG_BIG = -1e30
LOG2E = math.log2(math.e)
MOD_ROWS = 8
V7X_VMEM_LIMIT = 56 * 1024 * 1024

F32 = jnp.float32
BF16 = jnp.bfloat16


def _cparams(sem):
    return pltpu.CompilerParams(dimension_semantics=sem, vmem_limit_bytes=V7X_VMEM_LIMIT)


def _pick(n, candidates):
    for c in candidates:
        if n % c == 0:
            return c
    raise ValueError(f"no tile in {candidates} divides {n}")


def _mod_kernel(c_ref, w_ref, b_ref, o_ref):
    c = c_ref[...]
    a = (c * jax.nn.sigmoid(c)).astype(BF16)
    o_ref[0] = jnp.dot(a, w_ref[0].astype(BF16), preferred_element_type=F32) + b_ref[0]


def _mod_vectors(c_rows, w_mod, b_mod):
    depth, d, n = w_mod.shape
    tn = _pick(n, (1024, 512, 256, 128))
    return pl.pallas_call(
        _mod_kernel,
        grid=(depth, n // tn),
        in_specs=[
            pl.BlockSpec((MOD_ROWS, d), lambda l, j: (0, 0)),
            pl.BlockSpec((1, d, tn), lambda l, j: (l, 0, j)),
            pl.BlockSpec((1, 1, tn), lambda l, j: (l, 0, j)),
        ],
        out_specs=pl.BlockSpec((1, MOD_ROWS, tn), lambda l, j: (l, 0, j)),
        out_shape=jax.ShapeDtypeStruct((depth, MOD_ROWS, n), F32),
        compiler_params=_cparams(("parallel", "parallel")),
        name="mod_vectors",
    )(c_rows, w_mod, b_mod.reshape(depth, 1, n))


def _cast_kernel(w_ref, o_ref):
    o_ref[...] = w_ref[...].astype(BF16)


def _to_bf16(w):
    depth, rows, cols = w.shape
    tr = _pick(rows, (256, 128))
    return pl.pallas_call(
        _cast_kernel,
        grid=(depth, rows // tr),
        in_specs=[pl.BlockSpec((None, tr, cols), lambda l, i: (l, i, 0))],
        out_specs=pl.BlockSpec((None, tr, cols), lambda l, i: (l, i, 0)),
        out_shape=jax.ShapeDtypeStruct(w.shape, BF16),
        compiler_params=_cparams(("parallel", "parallel")),
        name="weights_to_bf16",
    )(w)


def _rope_tables(seq, head_width):
    quarter = head_width // 4
    lane = np.arange(HEAD_DIM)
    within_head = lane % head_width
    use_col = (within_head // (2 * quarter)) == 1
    within_half = within_head % (2 * quarter)
    first = within_half < quarter
    freq = within_half % quarter
    t = jnp.arange(seq)
    row, col = t // GRID_W, t % GRID_W
    inv = (ROPE_THETA ** (-jnp.arange(quarter, dtype=F32) / quarter))[freq]
    pos = jnp.where(use_col[None, :], col.astype(F32)[:, None], row.astype(F32)[:, None])
    ang = pos * inv[None, :]
    cos, sin = jnp.cos(ang), jnp.sin(ang)
    zero = jnp.zeros_like(sin)
    return jnp.stack([cos, jnp.where(first[None, :], -sin, zero), jnp.where(first[None, :], zero, sin)])


def _rope(y, tab_ref, quarter):
    up = pltpu.roll(y, HEAD_DIM - quarter, 1)
    down = pltpu.roll(y, quarter, 1)
    return y * tab_ref[0] + up * tab_ref[1] + down * tab_ref[2]


def _col_plan(sizes):
    a_qk, _, a_v, b_q, b_kv, _, c_w, _, _ = sizes
    u = lambda width: width // HEAD_DIM
    s_a = (HEAD_DIM // 2) ** -0.5 * LOG2E
    s_b = HEAD_DIM ** -0.5 * LOG2E
    t_qb, t_va, t_vb = u(a_qk), u(a_qk) + u(b_q), u(a_qk) + u(b_q) + u(a_v)
    plan = ([("a", None, s_a, i) for i in range(u(a_qk))] + [("a", None, None, None)] * u(a_qk)
            + [(None, None, None, t_va + i) for i in range(u(a_v))]
            + [("b", "q", s_b, t_qb + i) for i in range(u(b_q))] + [("b", "k", None, None)] * u(b_kv)
            + [(None, None, None, t_vb + i) for i in range(u(b_kv))]
            + [(None, None, s_b, None)] * u(c_w) + [(None, None, None, None)] * (2 * u(c_w)))
    return plan


def _inproj_kernel(x_ref, mod_ref, g_ref, w_ref, gq_ref, gk_ref, ta_ref, tb_ref, o_ref, vt_ref, *,
                   plan, chunk_units, use_rope):
    x = x_ref[...]
    m = mod_ref[...]
    ms = jnp.mean(x * x, axis=-1, keepdims=True)
    xn = x * lax.rsqrt(ms + EPS) * g_ref[...]
    xm = (xn * (1.0 + m[1:2]) + m[0:1]).astype(BF16)
    n_units = len(plan)
    for u0 in range(0, n_units, chunk_units):
        u1 = min(u0 + chunk_units, n_units)
        y = jnp.dot(xm, w_ref[:, u0 * HEAD_DIM:u1 * HEAD_DIM], preferred_element_type=F32)
        for u in range(u0, u1):
            yu = y[:, (u - u0) * HEAD_DIM:(u - u0 + 1) * HEAD_DIM]
            rope, norm, q_scale, vt_slot = plan[u]
            if norm is not None:
                gain = gq_ref[...] if norm == "q" else gk_ref[...]
                yu = yu * lax.rsqrt(jnp.mean(yu * yu, axis=-1, keepdims=True) + EPS) * gain
            if rope is not None and use_rope:
                yu = _rope(yu, ta_ref, HEAD_DIM // 8) if rope == "a" else _rope(yu, tb_ref, HEAD_DIM // 4)
            if q_scale is not None:
                yu = yu * q_scale
            o_ref[:, u * HEAD_DIM:(u + 1) * HEAD_DIM] = yu.astype(BF16)
            if vt_slot is not None:
                vt_ref[vt_slot * HEAD_DIM:(vt_slot + 1) * HEAD_DIM, :] = yu.T.astype(BF16)


def _in_projection(h2d, mods, mod_row_of_block, g_pre, w_bf16, g_qn, g_kn, tabs_a, tabs_b, *,
                   layer, sizes, batch, seq, tm, use_rope):
    rows, d = h2d.shape
    n = w_bf16.shape[2]
    plan = _col_plan(sizes)
    n_vt = sum(1 for p in plan if p[3] is not None)
    bps = seq // tm
    kern = functools.partial(_inproj_kernel, plan=plan, chunk_units=4, use_rope=use_rope)
    return pl.pallas_call(
        kern,
        grid=(rows // tm,),
        in_specs=[
            pl.BlockSpec((tm, d), lambda i: (i, 0)),
            pl.BlockSpec((None, 6, d), lambda i: (mod_row_of_block(i), 0, 0)),
            pl.BlockSpec((1, d), lambda i: (0, 0)),
            pl.BlockSpec((None, d, n), lambda i: (layer, 0, 0), pipeline_mode=pl.Buffered(1)),
            pl.BlockSpec((1, HEAD_DIM), lambda i: (0, 0)),
            pl.BlockSpec((1, HEAD_DIM), lambda i: (0, 0)),
            pl.BlockSpec((3, tm, HEAD_DIM), lambda i: (0, i % bps, 0)),
            pl.BlockSpec((3, tm, HEAD_DIM), lambda i: (0, i % bps, 0)),
        ],
        out_specs=[pl.BlockSpec((tm, n), lambda i: (i, 0)),
                   pl.BlockSpec((None, n_vt * HEAD_DIM, tm), lambda i: (i // bps, 0, i % bps))],
        out_shape=[jax.ShapeDtypeStruct((rows, n), BF16),
                   jax.ShapeDtypeStruct((batch, n_vt * HEAD_DIM, seq), BF16)],
        compiler_params=_cparams(("parallel",)),
        name="in_projection",
    )(h2d, mods, g_pre, w_bf16, g_qn, g_kn, tabs_a, tabs_b)


def _fold_rows(x, op):
    slabs = [x[r:r + 8] for r in range(0, x.shape[0], 8)]
    while len(slabs) > 1:
        nxt = [op(slabs[i], slabs[i + 1]) for i in range(0, len(slabs) - 1, 2)]
        if len(slabs) % 2:
            nxt.append(slabs[-1])
        slabs = nxt
    return slabs[0]


def _scores(q_list, k):
    out = []
    for qt, u, _ in q_list:
        s = jnp.dot(k[:, u * HEAD_DIM:(u + 1) * HEAD_DIM], qt, preferred_element_type=F32)
        out.append((s, _fold_rows(s, jnp.maximum)))
    return out


def _softmax_pv(stats, scores, q_list, vt, acc_ref):
    new = []
    for c, ((m, l), (s, smax), (_, _, v_unit)) in enumerate(zip(stats, scores, q_list)):
        m_new = jnp.maximum(m, jnp.max(smax, axis=0, keepdims=True))
        alpha = jnp.exp2(m - m_new)
        p = jnp.exp2(s - m_new)
        l = alpha * l + jnp.sum(_fold_rows(p, jnp.add), axis=0, keepdims=True)
        vv = vt[v_unit * HEAD_DIM:(v_unit + 1) * HEAD_DIM, :]
        acc_ref[c] = alpha * acc_ref[c] + jnp.dot(vv, p.astype(BF16), preferred_element_type=F32)
        new.append((m_new, l))
    return tuple(new)


def _attn_chains(q_lists, sources, tq, tk, s_refs, mx_refs, acc_ref, finish):
    n_ch = len(q_lists[0])
    kc_ref, vtc_ref = sources[0]
    latent = len(sources) == 2
    if latent:
        k_ref, vt_ref = sources[1]
        n = k_ref.shape[0] // tk
        assert n >= 2 and n % 2 == 0 and n * tk == k_ref.shape[0]

        def keys_of(c):
            return pl.ds(c * tk if isinstance(c, int) else pl.multiple_of(c * tk, tk), tk)

        def put(q_list, slot, c):
            for i, (s, smax) in enumerate(_scores(q_list, k_ref[keys_of(c), :])):
                s_refs[slot][i] = s
                mx_refs[slot][i] = smax

        def take(q_list, acc, st, slot, c):
            return _softmax_pv(st, [(s_refs[slot][i], mx_refs[slot][i]) for i in range(n_ch)], q_list,
                               vt_ref[:, keys_of(c)], acc)

        put(q_lists[0], 0, 0)

    for j, q_list in enumerate(q_lists):
        acc = acc_ref.at[j % 2]
        acc[...] = jnp.zeros(acc.shape, F32)
        stats = tuple((jnp.full((1, tq), NEG_BIG, F32), jnp.zeros((1, tq), F32)) for _ in q_list)
        if latent:
            def body(i, st, q_list=q_list, acc=acc):
                a = 2 * i
                put(q_list, 1, a + 1)
                st = take(q_list, acc, st, 0, a)
                put(q_list, 0, a + 2)
                return take(q_list, acc, st, 1, a + 1)

            stats = lax.fori_loop(0, n // 2 - 1, body, stats)
            put(q_list, 1, n - 1)
            stats = take(q_list, acc, stats, 0, n - 2)
            s_ctx = _scores(q_list, kc_ref[...])
            if j + 1 < len(q_lists):
                put(q_lists[j + 1], 0, 0)
            stats = take(q_list, acc, stats, 1, n - 1)
        else:
            s_ctx = _scores(q_list, kc_ref[...])
        stats = _softmax_pv(stats, s_ctx, q_list, vtc_ref[...], acc)
        finish(j, [(acc[c] / l).T for c, (_, l) in enumerate(stats)])


def _attn_scratch(n_chains, tq, tk):
    scores = pltpu.VMEM((n_chains, tk, tq), F32)
    maxima = pltpu.VMEM((n_chains, 8, tq), F32)
    return [scores, scores, maxima, maxima, pltpu.VMEM((2, n_chains, HEAD_DIM, tq), F32)]


def _gqa_kernel(*refs, group, tq, tk, n_sources):
    qt_ref = refs[0]
    kv = refs[1:1 + 2 * n_sources]
    o_ref, s0_ref, s1_ref, mx0_ref, mx1_ref, acc_ref = refs[1 + 2 * n_sources:]
    sources = [(kv[2 * s], kv[2 * s + 1]) for s in range(n_sources)]
    n_sub = qt_ref.shape[1] // tq
    q_lists = [[(qt_ref[g * HEAD_DIM:(g + 1) * HEAD_DIM, j * tq:(j + 1) * tq], 0, 0)
                for g in range(group)] for j in range(n_sub)]

    def finish(j, outs):
        for g, o in enumerate(outs):
            o_ref[j * tq:(j + 1) * tq, g * HEAD_DIM:(g + 1) * HEAD_DIM] = o.astype(BF16)

    _attn_chains(q_lists, sources, tq, tk, (s0_ref, s1_ref), (mx0_ref, mx1_ref), acc_ref, finish)


def _gqa_attention(qt_arr, qt_unit, kv_arrs, k_col, vt_unit, *, batch, sq, n_kv, group, tq, tq_sub, tk,
                   name):
    nq = sq // tq
    assert tq % tq_sub == 0
    gw = group * HEAD_DIM
    assert qt_unit % group == 0
    in_specs = [pl.BlockSpec((None, gw, tq), lambda b, h, i: (b, qt_unit // group + h, i))]
    args = [qt_arr]
    for k_arr, vt_arr, t in kv_arrs:
        in_specs.append(pl.BlockSpec((t, HEAD_DIM), lambda b, h, i: (b, k_col + h)))
        in_specs.append(pl.BlockSpec((None, HEAD_DIM, t), lambda b, h, i: (b, vt_unit + h, 0)))
        args += [k_arr, vt_arr]
    kern = functools.partial(_gqa_kernel, group=group, tq=tq_sub, tk=tk, n_sources=len(kv_arrs))
    return pl.pallas_call(
        kern,
        grid=(batch, n_kv, nq),
        in_specs=in_specs,
        out_specs=pl.BlockSpec((tq, gw), lambda b, h, i: (b * nq + i, h)),
        out_shape=jax.ShapeDtypeStruct((batch * sq, n_kv * gw), BF16),
        scratch_shapes=_attn_scratch(group, tq_sub, tk),
        compiler_params=_cparams(("parallel", "parallel", "parallel")),
        name=name,
    )(*args)


def _diff_kernel(*refs, hps, tq, tk, lam_init, n_sources):
    qt_ref, lq1, lk1, lq2, lk2, g_ref = refs[:6]
    kv = refs[6:6 + 2 * n_sources]
    o_ref, s0_ref, s1_ref, mx0_ref, mx1_ref, acc_ref = refs[6 + 2 * n_sources:]
    sources = [(kv[2 * s], kv[2 * s + 1]) for s in range(n_sources)]
    n_sub = qt_ref.shape[1] // tq
    half = HEAD_DIM // 2
    lam = (jnp.exp(jnp.sum(lq1[...] * lk1[...], axis=-1, keepdims=True))
           - jnp.exp(jnp.sum(lq2[...] * lk2[...], axis=-1, keepdims=True)) + lam_init)
    dim = lax.broadcasted_iota(jnp.int32, (HEAD_DIM, tq), 0)
    q_lists = []
    for j in range(n_sub):
        q_list = []
        for h in range(hps):
            qt = qt_ref[h * HEAD_DIM:(h + 1) * HEAD_DIM, j * tq:(j + 1) * tq]
            zero = jnp.zeros_like(qt)
            q_list.append((jnp.where(dim < half, qt, zero), h, h))
            q_list.append((jnp.where(dim < half, zero, qt), h, h))
        q_lists.append(q_list)

    def finish(j, outs):
        for h in range(hps):
            o = outs[2 * h] - lam * outs[2 * h + 1]
            o = o * lax.rsqrt(jnp.mean(o * o, axis=-1, keepdims=True) + EPS) * g_ref[...]
            o_ref[j * tq:(j + 1) * tq, h * HEAD_DIM:(h + 1) * HEAD_DIM] = (o * (1.0 - lam_init)).astype(BF16)

    _attn_chains(q_lists, sources, tq, tk, (s0_ref, s1_ref), (mx0_ref, mx1_ref), acc_ref, finish)


def _diff_attention(qt_arr, kv_arrs, lams, g_diff, *, batch, sq, heads, qt_unit, k_col, vt_unit,
                    hps, tq, tq_sub, tk, lam_init, name):
    nq = sq // tq
    assert tq % tq_sub == 0
    half = HEAD_DIM // 2
    w = hps * HEAD_DIM
    assert heads % hps == 0 and qt_unit % hps == 0 and k_col % hps == 0 and vt_unit % hps == 0
    small = lambda width: pl.BlockSpec((1, width), lambda b, h, i: (0, 0))
    in_specs = [pl.BlockSpec((None, w, tq), lambda b, h, i: (b, qt_unit // hps + h, i)),
                small(half), small(half), small(half), small(half), small(HEAD_DIM)]
    args = [qt_arr, *lams, g_diff]
    for k_arr, vt_arr, t in kv_arrs:
        in_specs.append(pl.BlockSpec((t, w), lambda b, h, i: (b, k_col // hps + h)))
        in_specs.append(pl.BlockSpec((None, w, t), lambda b, h, i: (b, vt_unit // hps + h, 0)))
        args += [k_arr, vt_arr]
    kern = functools.partial(_diff_kernel, hps=hps, tq=tq_sub, tk=tk, lam_init=lam_init,
                             n_sources=len(kv_arrs))
    return pl.pallas_call(
        kern,
        grid=(batch, heads // hps, nq),
        in_specs=in_specs,
        out_specs=pl.BlockSpec((tq, w), lambda b, h, i: (b * nq + i, h)),
        out_shape=jax.ShapeDtypeStruct((batch * sq, heads * HEAD_DIM), BF16),
        scratch_shapes=_attn_scratch(2 * hps, tq_sub, tk),
        compiler_params=_cparams(("parallel", "parallel", "parallel")),
        name=name,
    )(*args)


def _na_plan(seq, tq, wr_max, win_rows):
    rows_n = seq // GRID_W
    wr = min(wr_max, rows_n)
    r_per = tq // GRID_W
    outside = 2 * wr_max - 1
    starts, pats = [], []
    for blk in range(seq // tq):
        rf = blk * r_per
        lo = int(np.clip(rf - wr // 2, 0, rows_n - wr))
        w0 = min(lo, rows_n - win_rows)
        r = rf + np.arange(r_per)
        kr = w0 + np.arange(win_rows)
        r0 = np.clip(r - wr // 2, 0, rows_n - wr)
        row_ok = (kr[None, :] >= r0[:, None]) & (kr[None, :] < r0[:, None] + wr)
        roff = kr[None, :] - r[:, None] + wr_max - 1
        starts.append(w0 * GRID_W)
        pats.append(np.where(row_ok, roff, outside))
    uniq, pids = [], []
    for p in pats:
        for j, u in enumerate(uniq):
            if np.array_equal(u, p):
                pids.append(j)
                break
        else:
            pids.append(len(uniq))
            uniq.append(p)
    return np.asarray(starts, np.int32), np.asarray(pids, np.int32), np.stack(uniq)


def _na_bias(rpb_l, row_pats, tq, wk):
    heads = rpb_l.shape[0]
    cidx = np.arange(GRID_W)
    c0 = np.clip(cidx - NA_COLS // 2, 0, GRID_W - NA_COLS)
    col_ok = (cidx[None, :] >= c0[:, None]) & (cidx[None, :] < c0[:, None] + NA_COLS)
    coff = np.clip(cidx[None, :] - cidx[:, None], -(NA_COLS - 1), NA_COLS - 1) + NA_COLS - 1
    n_coff = rpb_l.shape[2]
    onehot = jnp.asarray(coff.reshape(-1)[None, :] == np.arange(n_coff)[:, None], F32)
    picked = jnp.dot(rpb_l.astype(F32).reshape(-1, n_coff), onehot, precision=lax.Precision.HIGHEST)
    picked = picked.reshape(heads, rpb_l.shape[1], GRID_W, GRID_W)
    by_col = jnp.where(col_ok[None, None], picked * LOG2E, NEG_BIG)
    outside = jnp.full((heads, GRID_W, GRID_W), NEG_BIG, F32)
    n_rel = by_col.shape[1]
    tile = lambda rel: by_col[:, rel] if rel < n_rel else outside
    strips = [jnp.concatenate([tile(int(rel)) for rel in q_row], axis=-1)
              for pat in row_pats for q_row in pat]
    n_pat = row_pats.shape[0]
    return jnp.stack(strips, axis=1).reshape(heads, n_pat, tq, wk)


def _na_kernel(w0_ref, pid_ref, q_ref, kc_ref, vc_ref, k_ref, v_ref, bias_ref, o_ref, *, heads, wk):
    del pid_ref
    w0 = pl.multiple_of(w0_ref[pl.program_id(1)], GRID_W)
    win = pl.ds(w0, wk)
    nt = (((1,), (1,)), ((), ()))
    head_lanes = [slice(h * HEAD_DIM, (h + 1) * HEAD_DIM) for h in range(heads)]
    scores = []
    for h, lanes in enumerate(head_lanes):
        q = q_ref[:, lanes]
        s_w = lax.dot_general(q, k_ref[win, lanes], nt, preferred_element_type=F32) + bias_ref[h]
        s_c = lax.dot_general(q, kc_ref[:, lanes], nt, preferred_element_type=F32)
        scores.append((s_w, s_c))
    probs = []
    for s_w, s_c in scores:
        m = jnp.maximum(jnp.max(s_w, axis=-1, keepdims=True), jnp.max(s_c, axis=-1, keepdims=True))
        p_w = jnp.exp2(s_w - m)
        p_c = jnp.exp2(s_c - m)
        l = jnp.sum(p_w, axis=-1, keepdims=True) + jnp.sum(p_c, axis=-1, keepdims=True)
        probs.append((p_w.astype(BF16), p_c.astype(BF16), l))
    for (p_w, p_c, l), lanes in zip(probs, head_lanes):
        acc = (jnp.dot(p_w, v_ref[win, lanes], preferred_element_type=F32)
               + jnp.dot(p_c, vc_ref[:, lanes], preferred_element_type=F32))
        o_ref[:, lanes] = (acc / l).astype(BF16)


def _na_attention(qkv, qkv_c, rpb_l, *, batch, seq, ctx_len, heads, q_col, k_col, v_col, tq):
    wr_max = (rpb_l.shape[1] + 1) // 2
    r_per = tq // GRID_W
    win_rows = min(-(-(r_per + wr_max - 1) // 4) * 4, seq // GRID_W)
    wk = win_rows * GRID_W
    starts, pids, row_pats = _na_plan(seq, tq, wr_max, win_rows)
    bias = _na_bias(rpb_l, row_pats, tq, wk)
    nq = seq // tq
    w = heads * HEAD_DIM
    kern = functools.partial(_na_kernel, heads=heads, wk=wk)
    grid_spec = pltpu.PrefetchScalarGridSpec(
        num_scalar_prefetch=2,
        grid=(batch, nq),
        in_specs=[
            pl.BlockSpec((tq, w), lambda b, i, w0, pid: (b * nq + i, q_col // heads)),
            pl.BlockSpec((ctx_len, w), lambda b, i, w0, pid: (b, k_col // heads)),
            pl.BlockSpec((ctx_len, w), lambda b, i, w0, pid: (b, v_col // heads)),
            pl.BlockSpec((seq, w), lambda b, i, w0, pid: (b, k_col // heads)),
            pl.BlockSpec((seq, w), lambda b, i, w0, pid: (b, v_col // heads)),
            pl.BlockSpec((heads, None, tq, wk), lambda b, i, w0, pid: (0, pid[i], 0, 0)),
        ],
        out_specs=pl.BlockSpec((tq, w), lambda b, i, w0, pid: (b * nq + i, 0)),
    )
    return pl.pallas_call(
        kern,
        grid_spec=grid_spec,
        out_shape=jax.ShapeDtypeStruct((batch * seq, w), BF16),
        compiler_params=_cparams(("parallel", "parallel")),
        name="na_attention",
    )(jnp.asarray(starts), jnp.asarray(pids), qkv, qkv_c, qkv_c, qkv, qkv, bias)


def _dense_ctx_kernel(q_ref, k_ref, v_ref, o_ref, *, heads):
    for h in range(heads):
        lanes = slice(h * HEAD_DIM, (h + 1) * HEAD_DIM)
        s = lax.dot_general(q_ref[:, lanes], k_ref[:, lanes], (((1,), (1,)), ((), ())),
                            preferred_element_type=F32)
        p = jnp.exp2(s - jnp.max(s, axis=-1, keepdims=True))
        acc = jnp.dot(p.astype(BF16), v_ref[:, lanes], preferred_element_type=F32)
        o_ref[:, lanes] = (acc / jnp.sum(p, axis=-1, keepdims=True)).astype(BF16)


def _dense_ctx_attention(qkv_c, *, batch, ctx_len, heads, q_col, k_col, v_col):
    w = heads * HEAD_DIM
    spec = lambda col: pl.BlockSpec((ctx_len, w), lambda b: (b, col // heads))
    return pl.pallas_call(
        functools.partial(_dense_ctx_kernel, heads=heads),
        grid=(batch,),
        in_specs=[spec(q_col), spec(k_col), spec(v_col)],
        out_specs=pl.BlockSpec((ctx_len, w), lambda b: (b, 0)),
        out_shape=jax.ShapeDtypeStruct((batch * ctx_len, w), BF16),
        compiler_params=_cparams(("parallel",)),
        name="dense_attention_ctx",
    )(qkv_c, qkv_c, qkv_c)


def _outproj_kernel(oa_ref, ob_ref, on_ref, w_ref, h_ref, mod_ref, g_ref, o_ref):
    wa, wb = oa_ref.shape[1], ob_ref.shape[1]
    tm = o_ref.shape[0]
    slab = min(tm, 2 * HEAD_DIM)
    for r0 in range(0, tm, slab):
        rows = slice(r0, r0 + slab)
        y = jnp.dot(oa_ref[rows, :], w_ref[:wa, :], preferred_element_type=F32)
        y += jnp.dot(ob_ref[rows, :], w_ref[wa:wa + wb, :], preferred_element_type=F32)
        y += jnp.dot(on_ref[rows, :], w_ref[wa + wb:, :], preferred_element_type=F32)
        yn = y * lax.rsqrt(jnp.mean(y * y, axis=-1, keepdims=True) + EPS) * g_ref[...]
        o_ref[rows, :] = h_ref[rows, :] + mod_ref[2:3] * yn


def _out_projection(oa, ob, on, w_bf16, h2d, mods, mod_row_of_block, g_post, *, layer, tm):
    rows, d = h2d.shape
    kdim = w_bf16.shape[1]
    return pl.pallas_call(
        _outproj_kernel,
        grid=(rows // tm,),
        in_specs=[
            pl.BlockSpec((tm, oa.shape[1]), lambda i: (i, 0)),
            pl.BlockSpec((tm, ob.shape[1]), lambda i: (i, 0)),
            pl.BlockSpec((tm, on.shape[1]), lambda i: (i, 0)),
            pl.BlockSpec((None, kdim, d), lambda i: (layer, 0, 0), pipeline_mode=pl.Buffered(1)),
            pl.BlockSpec((tm, d), lambda i: (i, 0)),
            pl.BlockSpec((None, 6, d), lambda i: (mod_row_of_block(i), 0, 0)),
            pl.BlockSpec((1, d), lambda i: (0, 0)),
        ],
        out_specs=pl.BlockSpec((tm, d), lambda i: (i, 0)),
        out_shape=jax.ShapeDtypeStruct((rows, d), F32),
        compiler_params=_cparams(("parallel",)),
        name="out_projection",
    )(oa, ob, on, w_bf16, h2d, mods, g_post)


def _ffn_kernel(h_ref, mod_ref, gpre_ref, wg_ref, wu_ref, wd_ref, gpost_ref, o_ref, xm_ref):
    f = pl.program_id(1)
    last = pl.num_programs(1) - 1
    tm = o_ref.shape[0]
    slab = min(tm, 2 * HEAD_DIM)

    def swiglu_down(xm):
        gate = jnp.dot(xm, wg_ref[...], preferred_element_type=F32)
        up = jnp.dot(xm, wu_ref[...], preferred_element_type=F32)
        act = (gate * jax.nn.sigmoid(gate) * up).astype(BF16)
        return jnp.dot(act, wd_ref[...], preferred_element_type=F32)

    @pl.when(f == 0)
    def _():
        for r0 in range(0, tm, slab):
            rows = slice(r0, r0 + slab)
            ms = jnp.mean(jnp.square(h_ref[rows, :]), axis=-1, keepdims=True)
            xn = h_ref[rows, :] * lax.rsqrt(ms + EPS) * gpre_ref[...]
            xm = (xn * (1.0 + mod_ref[4:5]) + mod_ref[3:4]).astype(BF16)
            xm_ref[rows, :] = xm
            o_ref[rows, :] = swiglu_down(xm)

    @pl.when(jnp.logical_and(f > 0, f < last))
    def _():
        o_ref[...] += swiglu_down(xm_ref[...])

    @pl.when(f == last)
    def _():
        for r0 in range(0, tm, slab):
            rows = slice(r0, r0 + slab)
            o_ref[rows, :] += swiglu_down(xm_ref[rows, :])
            ms = jnp.mean(jnp.square(o_ref[rows, :]), axis=-1, keepdims=True)
            yn = o_ref[rows, :] * lax.rsqrt(ms + EPS) * gpost_ref[...]
            o_ref[rows, :] = h_ref[rows, :] + mod_ref[5:6] * yn


def _ffn(h2d, mods, mod_row_of_block, g_pre, wg, wu, wd, g_post, *, layer, tm, tf):
    rows, d = h2d.shape
    ff = wg.shape[2]
    assert ff // tf >= 2
    return pl.pallas_call(
        _ffn_kernel,
        grid=(rows // tm, ff // tf),
        in_specs=[
            pl.BlockSpec((tm, d), lambda i, f: (i, 0)),
            pl.BlockSpec((None, 6, d), lambda i, f: (mod_row_of_block(i), 0, 0)),
            pl.BlockSpec((1, d), lambda i, f: (0, 0)),
            pl.BlockSpec((None, d, tf), lambda i, f: (layer, 0, f)),
            pl.BlockSpec((None, d, tf), lambda i, f: (layer, 0, f)),
            pl.BlockSpec((None, tf, d), lambda i, f: (layer, f, 0)),
            pl.BlockSpec((1, d), lambda i, f: (0, 0)),
        ],
        out_specs=pl.BlockSpec((tm, d), lambda i, f: (i, 0)),
        out_shape=jax.ShapeDtypeStruct((rows, d), F32),
        scratch_shapes=[pltpu.VMEM((tm, d), BF16)],
        compiler_params=_cparams(("parallel", "arbitrary")),
        name="ffn",
    )(h2d, mods, g_pre, wg, wu, wd, g_post)


def kernel(x, c, ctx, c_ctx, w_mod, b_mod, g_pre1, g_post1, g_pre2, g_post2, w_in, w_out,
           lam_q1, lam_k1, lam_q2, lam_k2, g_diff, g_qn, g_kn, rpb, w_gate, w_up, w_down):
    batch, seq, d = x.shape
    ctx_len = ctx.shape[1]
    depth = w_mod.shape[0]
    n_heads = d // HEAD_DIM
    a_heads, b_heads, c_heads = n_heads // 4, n_heads // 2, n_heads // 4
    b_kv = b_heads // 4
    group = b_heads // b_kv
    a_qk = a_v = a_heads * HEAD_DIM
    b_q, b_kvw, c_w = b_heads * HEAD_DIM, b_kv * HEAD_DIM, c_heads * HEAD_DIM
    sizes = (a_qk, a_qk, a_v, b_q, b_kvw, b_kvw, c_w, c_w, c_w)
    starts = np.cumsum((0,) + sizes)[:-1] // HEAD_DIM
    _, ka_c, _, _, kb_c, _, qn_c, kn_c, vn_c = (int(s) for s in starts)
    qta_u, qtb_u = 0, a_heads
    vta_u, vtb_u = a_heads + b_heads, 2 * a_heads + b_heads
    assert batch + 1 <= MOD_ROWS and seq % GRID_W == 0

    tm = _pick(seq, (512, 256, 128))
    tm_c = _pick(ctx_len, (512, 256, 128))
    tq = _pick(seq, (256, 128))
    tq_dense = _pick(seq, (2048, 1024, 512, 256, 128))
    tq_sub = min(tq_dense, 1024)
    tq_c = _pick(ctx_len, (256, 128))
    tk = 512
    tf = _pick(w_gate.shape[2], (512, 256, 128))
    tm_ffn = _pick(seq, (1024, 512, 256, 128))
    tm_ffn_c = _pick(batch * ctx_len, (1024, 512, 256, 128))
    hps = 2 if a_heads % 2 == 0 else 1

    c_rows = jnp.zeros((MOD_ROWS, d), F32).at[:batch].set(c).at[batch].set(c_ctx)
    mods = _mod_vectors(c_rows, w_mod, b_mod).reshape(depth, MOD_ROWS, 6, d)
    tabs_a = _rope_tables(seq, HEAD_DIM // 2)
    tabs_b = _rope_tables(seq, HEAD_DIM)
    tabs_c = jnp.zeros((3, tm_c, HEAD_DIM), F32)

    lat_row = lambda i: i // (seq // tm)
    ctx_row = lambda i: batch
    row1 = lambda v: v.reshape(1, -1)

    w_in_b, w_out_b = _to_bf16(w_in), _to_bf16(w_out)
    wg, wu, wd = _to_bf16(w_gate), _to_bf16(w_up), _to_bf16(w_down)

    h = x.reshape(batch * seq, d)
    hc = ctx.reshape(batch * ctx_len, d)
    for l in range(depth):
        need_ctx = l < depth - 1
        lam_init = 0.8 - 0.6 * math.exp(-0.3 * l)
        lams = [row1(v[l]) for v in (lam_q1, lam_k1, lam_q2, lam_k2)]
        inproj = functools.partial(_in_projection, g_pre=row1(g_pre1[l]), w_bf16=w_in_b, layer=l,
                                   g_qn=row1(g_qn[l]), g_kn=row1(g_kn[l]), sizes=sizes, batch=batch)
        qkv, vt = inproj(h, mods[l], lat_row, tabs_a=tabs_a, tabs_b=tabs_b, seq=seq, tm=tm, use_rope=True)
        qkv_c, vt_c = inproj(hc, mods[l], ctx_row, tabs_a=tabs_c, tabs_b=tabs_c, seq=ctx_len, tm=tm_c,
                             use_rope=False)
        both = [(qkv_c, vt_c, ctx_len), (qkv, vt, seq)]
        only_ctx = [(qkv_c, vt_c, ctx_len)]

        oa = _diff_attention(vt, both, lams, row1(g_diff[l]), batch=batch, sq=seq, heads=a_heads,
                             qt_unit=qta_u, k_col=ka_c, vt_unit=vta_u, hps=hps, tq=tq_dense,
                             tq_sub=tq_sub, tk=tk,
                             lam_init=lam_init, name="diff_attention")
        ob = _gqa_attention(vt, qtb_u, both, kb_c, vtb_u, batch=batch, sq=seq, n_kv=b_kv, group=group,
                            tq=tq_dense, tq_sub=tq_sub, tk=tk, name="gqa_attention")
        on = _na_attention(qkv, qkv_c, rpb[l], batch=batch, seq=seq, ctx_len=ctx_len, heads=c_heads,
                           q_col=qn_c, k_col=kn_c, v_col=vn_c, tq=tq)
        h_mid = _out_projection(oa, ob, on, w_out_b, h, mods[l], lat_row, row1(g_post1[l]), layer=l, tm=tm)
        h_new = _ffn(h_mid, mods[l], lambda i: i // (seq // tm_ffn), row1(g_pre2[l]), wg, wu, wd,
                     row1(g_post2[l]), layer=l, tm=tm_ffn, tf=tf)

        if need_ctx:
            oac = _diff_attention(vt_c, only_ctx, lams, row1(g_diff[l]), batch=batch, sq=ctx_len,
                                  heads=a_heads, qt_unit=qta_u, k_col=ka_c, vt_unit=vta_u, hps=hps,
                                  tq=tq_c, tq_sub=tq_c, tk=tk, lam_init=lam_init,
                                  name="diff_attention_ctx")
            obc = _gqa_attention(vt_c, qtb_u, only_ctx, kb_c, vtb_u, batch=batch, sq=ctx_len, n_kv=b_kv,
                                 group=group, tq=tq_c, tq_sub=tq_c, tk=tk, name="gqa_attention_ctx")
            onc = _dense_ctx_attention(qkv_c, batch=batch, ctx_len=ctx_len, heads=c_heads, q_col=qn_c,
                                       k_col=kn_c, v_col=vn_c)
            hc_mid = _out_projection(oac, obc, onc, w_out_b, hc, mods[l], ctx_row, row1(g_post1[l]),
                                     layer=l, tm=tm_c)
            hc = _ffn(hc_mid, mods[l], ctx_row, row1(g_pre2[l]), wg, wu, wd, row1(g_post2[l]),
                      layer=l, tm=tm_ffn_c, tf=tf)
        h = h_new
    return h.reshape(batch, seq, d)
```

```python
import functools
import math

import numpy as np
import jax
import jax.numpy as jnp
from jax import lax
from jax.experimental import pallas as pl
from jax.experimental.pallas import tpu as pltpu

GRID_W = 64
HEAD_DIM = 128
NA_COLS = 16
ROPE_THETA = 10000.0
EPS = 1e-6
NEG_BIG = -1e30
LOG2E = math.log2(math.e)
MOD_ROWS = 8
V7X_VMEM_LIMIT = 56 * 1024 * 1024

F32 = jnp.float32
BF16 = jnp.bfloat16


def _cparams(sem):
    return pltpu.CompilerParams(dimension_semantics=sem, vmem_limit_bytes=V7X_VMEM_LIMIT)


def _pick(n, candidates):
    for c in candidates:
        if n % c == 0:
            return c
    raise ValueError(f"no tile in {candidates} divides {n}")


def _mod_kernel(c_ref, w_ref, b_ref, o_ref):
    c = c_ref[...]
    a = (c * jax.nn.sigmoid(c)).astype(BF16)
    o_ref[0] = jnp.dot(a, w_ref[0].astype(BF16), preferred_element_type=F32) + b_ref[0]


def _mod_vectors(c_rows, w_mod, b_mod):
    depth, d, n = w_mod.shape
    tn = _pick(n, (1024, 512, 256, 128))
    return pl.pallas_call(
        _mod_kernel,
        grid=(depth, n // tn),
        in_specs=[
            pl.BlockSpec((MOD_ROWS, d), lambda l, j: (0, 0)),
            pl.BlockSpec((1, d, tn), lambda l, j: (l, 0, j)),
            pl.BlockSpec((1, 1, tn), lambda l, j: (l, 0, j)),
        ],
        out_specs=pl.BlockSpec((1, MOD_ROWS, tn), lambda l, j: (l, 0, j)),
        out_shape=jax.ShapeDtypeStruct((depth, MOD_ROWS, n), F32),
        compiler_params=_cparams(("parallel", "parallel")),
        name="mod_vectors",
    )(c_rows, w_mod, b_mod.reshape(depth, 1, n))


def _cast_kernel(w_ref, o_ref):
    o_ref[...] = w_ref[...].astype(BF16)


def _to_bf16(w):
    depth, rows, cols = w.shape
    tr = _pick(rows, (256, 128))
    return pl.pallas_call(
        _cast_kernel,
        grid=(depth, rows // tr),
        in_specs=[pl.BlockSpec((None, tr, cols), lambda l, i: (l, i, 0))],
        out_specs=pl.BlockSpec((None, tr, cols), lambda l, i: (l, i, 0)),
        out_shape=jax.ShapeDtypeStruct(w.shape, BF16),
        compiler_params=_cparams(("parallel", "parallel")),
        name="weights_to_bf16",
    )(w)


def _rope_tables(seq, head_width):
    quarter = head_width // 4
    lane = np.arange(HEAD_DIM)
    within_head = lane % head_width
    use_col = (within_head // (2 * quarter)) == 1
    within_half = within_head % (2 * quarter)
    first = within_half < quarter
    freq = within_half % quarter
    t = jnp.arange(seq)
    row, col = t // GRID_W, t % GRID_W
    inv = (ROPE_THETA ** (-jnp.arange(quarter, dtype=F32) / quarter))[freq]
    pos = jnp.where(use_col[None, :], col.astype(F32)[:, None], row.astype(F32)[:, None])
    ang = pos * inv[None, :]
    cos, sin = jnp.cos(ang), jnp.sin(ang)
    zero = jnp.zeros_like(sin)
    return jnp.stack([cos, jnp.where(first[None, :], -sin, zero), jnp.where(first[None, :], zero, sin)])


def _rope(y, tab_ref, quarter):
    up = pltpu.roll(y, HEAD_DIM - quarter, 1)
    down = pltpu.roll(y, quarter, 1)
    return y * tab_ref[0] + up * tab_ref[1] + down * tab_ref[2]


def _col_plan(sizes):
    a_qk, _, a_v, b_q, b_kv, _, c_w, _, _ = sizes
    u = lambda width: width // HEAD_DIM
    s_a = (HEAD_DIM // 2) ** -0.5 * LOG2E
    s_b = HEAD_DIM ** -0.5 * LOG2E
    t_qb, t_va, t_vb = u(a_qk), u(a_qk) + u(b_q), u(a_qk) + u(b_q) + u(a_v)
    plan = ([("a", None, s_a, i) for i in range(u(a_qk))] + [("a", None, None, None)] * u(a_qk)
            + [(None, None, None, t_va + i) for i in range(u(a_v))]
            + [("b", "q", s_b, t_qb + i) for i in range(u(b_q))] + [("b", "k", None, None)] * u(b_kv)
            + [(None, None, None, t_vb + i) for i in range(u(b_kv))]
            + [(None, None, s_b, None)] * u(c_w) + [(None, None, None, None)] * (2 * u(c_w)))
    return plan


def _inproj_kernel(x_ref, mod_ref, g_ref, w_ref, gq_ref, gk_ref, ta_ref, tb_ref, o_ref, vt_ref, *,
                   plan, chunk_units, use_rope):
    x = x_ref[...]
    m = mod_ref[...]
    ms = jnp.mean(x * x, axis=-1, keepdims=True)
    xn = x * lax.rsqrt(ms + EPS) * g_ref[...]
    xm = (xn * (1.0 + m[1:2]) + m[0:1]).astype(BF16)
    n_units = len(plan)
    for u0 in range(0, n_units, chunk_units):
        u1 = min(u0 + chunk_units, n_units)
        y = jnp.dot(xm, w_ref[:, u0 * HEAD_DIM:u1 * HEAD_DIM], preferred_element_type=F32)
        for u in range(u0, u1):
            yu = y[:, (u - u0) * HEAD_DIM:(u - u0 + 1) * HEAD_DIM]
            rope, norm, q_scale, vt_slot = plan[u]
            if norm is not None:
                gain = gq_ref[...] if norm == "q" else gk_ref[...]
                yu = yu * lax.rsqrt(jnp.mean(yu * yu, axis=-1, keepdims=True) + EPS) * gain
            if rope is not None and use_rope:
                yu = _rope(yu, ta_ref, HEAD_DIM // 8) if rope == "a" else _rope(yu, tb_ref, HEAD_DIM // 4)
            if q_scale is not None:
                yu = yu * q_scale
            o_ref[:, u * HEAD_DIM:(u + 1) * HEAD_DIM] = yu.astype(BF16)
            if vt_slot is not None:
                vt_ref[vt_slot * HEAD_DIM:(vt_slot + 1) * HEAD_DIM, :] = yu.T.astype(BF16)


def _in_projection(h2d, mods, mod_row_of_block, g_pre, w_bf16, g_qn, g_kn, tabs_a, tabs_b, *,
                   layer, sizes, batch, seq, tm, use_rope):
    rows, d = h2d.shape
    n = w_bf16.shape[2]
    plan = _col_plan(sizes)
    n_vt = sum(1 for p in plan if p[3] is not None)
    bps = seq // tm
    kern = functools.partial(_inproj_kernel, plan=plan, chunk_units=4, use_rope=use_rope)
    return pl.pallas_call(
        kern,
        grid=(rows // tm,),
        in_specs=[
            pl.BlockSpec((tm, d), lambda i: (i, 0)),
            pl.BlockSpec((None, 6, d), lambda i: (mod_row_of_block(i), 0, 0)),
            pl.BlockSpec((1, d), lambda i: (0, 0)),
            pl.BlockSpec((None, d, n), lambda i: (layer, 0, 0), pipeline_mode=pl.Buffered(1)),
            pl.BlockSpec((1, HEAD_DIM), lambda i: (0, 0)),
            pl.BlockSpec((1, HEAD_DIM), lambda i: (0, 0)),
            pl.BlockSpec((3, tm, HEAD_DIM), lambda i: (0, i % bps, 0)),
            pl.BlockSpec((3, tm, HEAD_DIM), lambda i: (0, i % bps, 0)),
        ],
        out_specs=[pl.BlockSpec((tm, n), lambda i: (i, 0)),
                   pl.BlockSpec((None, n_vt * HEAD_DIM, tm), lambda i: (i // bps, 0, i % bps))],
        out_shape=[jax.ShapeDtypeStruct((rows, n), BF16),
                   jax.ShapeDtypeStruct((batch, n_vt * HEAD_DIM, seq), BF16)],
        compiler_params=_cparams(("parallel",)),
        name="in_projection",
    )(h2d, mods, g_pre, w_bf16, g_qn, g_kn, tabs_a, tabs_b)


def _fold_rows(x, op):
    slabs = [x[r:r + 8] for r in range(0, x.shape[0], 8)]
    while len(slabs) > 1:
        nxt = [op(slabs[i], slabs[i + 1]) for i in range(0, len(slabs) - 1, 2)]
        if len(slabs) % 2:
            nxt.append(slabs[-1])
        slabs = nxt
    return slabs[0]


def _scores(q_list, k):
    out = []
    for qt, u, _ in q_list:
        s = jnp.dot(k[:, u * HEAD_DIM:(u + 1) * HEAD_DIM], qt, preferred_element_type=F32)
        out.append((s, _fold_rows(s, jnp.maximum)))
    return out


def _softmax_pv(stats, scores, q_list, vt, acc_ref):
    new = []
    for c, ((m, l), (s, smax), (_, _, v_unit)) in enumerate(zip(stats, scores, q_list)):
        m_new = jnp.maximum(m, jnp.max(smax, axis=0, keepdims=True))
        alpha = jnp.exp2(m - m_new)
        p = jnp.exp2(s - m_new)
        l = alpha * l + jnp.sum(_fold_rows(p, jnp.add), axis=0, keepdims=True)
        vv = vt[v_unit * HEAD_DIM:(v_unit + 1) * HEAD_DIM, :]
        acc_ref[c] = alpha * acc_ref[c] + jnp.dot(vv, p.astype(BF16), preferred_element_type=F32)
        new.append((m_new, l))
    return tuple(new)


def _attn_chains(q_lists, sources, tq, tk, s_refs, mx_refs, acc_ref, finish):
    n_ch = len(q_lists[0])
    kc_ref, vtc_ref = sources[0]
    latent = len(sources) == 2
    if latent:
        k_ref, vt_ref = sources[1]
        n = k_ref.shape[0] // tk
        assert n >= 2 and n % 2 == 0 and n * tk == k_ref.shape[0]

        def keys_of(c):
            return pl.ds(c * tk if isinstance(c, int) else pl.multiple_of(c * tk, tk), tk)

        def put(q_list, slot, c):
            for i, (s, smax) in enumerate(_scores(q_list, k_ref[keys_of(c), :])):
                s_refs[slot][i] = s
                mx_refs[slot][i] = smax

        def take(q_list, acc, st, slot, c):
            return _softmax_pv(st, [(s_refs[slot][i], mx_refs[slot][i]) for i in range(n_ch)], q_list,
                               vt_ref[:, keys_of(c)], acc)

        put(q_lists[0], 0, 0)

    for j, q_list in enumerate(q_lists):
        acc = acc_ref.at[j % 2]
        acc[...] = jnp.zeros(acc.shape, F32)
        stats = tuple((jnp.full((1, tq), NEG_BIG, F32), jnp.zeros((1, tq), F32)) for _ in q_list)
        if latent:
            def body(i, st, q_list=q_list, acc=acc):
                a = 2 * i
                put(q_list, 1, a + 1)
                st = take(q_list, acc, st, 0, a)
                put(q_list, 0, a + 2)
                return take(q_list, acc, st, 1, a + 1)

            stats = lax.fori_loop(0, n // 2 - 1, body, stats)
            put(q_list, 1, n - 1)
            stats = take(q_list, acc, stats, 0, n - 2)
            s_ctx = _scores(q_list, kc_ref[...])
            if j + 1 < len(q_lists):
                put(q_lists[j + 1], 0, 0)
            stats = take(q_list, acc, stats, 1, n - 1)
        else:
            s_ctx = _scores(q_list, kc_ref[...])
        stats = _softmax_pv(stats, s_ctx, q_list, vtc_ref[...], acc)
        finish(j, [acc[c] / l for c, (_, l) in enumerate(stats)])


def _attn_scratch(n_chains, tq, tk):
    scores = pltpu.VMEM((n_chains, tk, tq), F32)
    maxima = pltpu.VMEM((n_chains, 8, tq), F32)
    return [scores, scores, maxima, maxima, pltpu.VMEM((2, n_chains, HEAD_DIM, tq), F32)]


def _gqa_kernel(*refs, group, tq, tk, n_sources):
    qt_ref = refs[0]
    kv = refs[1:1 + 2 * n_sources]
    o_ref, s0_ref, s1_ref, mx0_ref, mx1_ref, acc_ref = refs[1 + 2 * n_sources:]
    sources = [(kv[2 * s], kv[2 * s + 1]) for s in range(n_sources)]
    n_sub = qt_ref.shape[1] // tq
    q_lists = [[(qt_ref[g * HEAD_DIM:(g + 1) * HEAD_DIM, j * tq:(j + 1) * tq], 0, 0)
                for g in range(group)] for j in range(n_sub)]

    def finish(j, outs):
        for g, o in enumerate(outs):
            o_ref[j * tq:(j + 1) * tq, g * HEAD_DIM:(g + 1) * HEAD_DIM] = o.T.astype(BF16)

    _attn_chains(q_lists, sources, tq, tk, (s0_ref, s1_ref), (mx0_ref, mx1_ref), acc_ref, finish)


def _gqa_attention(qt_arr, qt_unit, kv_arrs, k_col, vt_unit, *, batch, sq, n_kv, group, tq, tq_sub, tk,
                   name):
    nq = sq // tq
    assert tq % tq_sub == 0
    gw = group * HEAD_DIM
    assert qt_unit % group == 0
    in_specs = [pl.BlockSpec((None, gw, tq), lambda b, h, i: (b, qt_unit // group + h, i))]
    args = [qt_arr]
    for k_arr, vt_arr, t in kv_arrs:
        in_specs.append(pl.BlockSpec((t, HEAD_DIM), lambda b, h, i: (b, k_col + h)))
        in_specs.append(pl.BlockSpec((None, HEAD_DIM, t), lambda b, h, i: (b, vt_unit + h, 0)))
        args += [k_arr, vt_arr]
    kern = functools.partial(_gqa_kernel, group=group, tq=tq_sub, tk=tk, n_sources=len(kv_arrs))
    return pl.pallas_call(
        kern,
        grid=(batch, n_kv, nq),
        in_specs=in_specs,
        out_specs=pl.BlockSpec((tq, gw), lambda b, h, i: (b * nq + i, h)),
        out_shape=jax.ShapeDtypeStruct((batch * sq, n_kv * gw), BF16),
        scratch_shapes=_attn_scratch(group, tq_sub, tk),
        compiler_params=_cparams(("parallel", "parallel", "parallel")),
        name=name,
    )(*args)


def _diff_kernel(*refs, hps, tq, tk, lam_init, n_sources):
    qt_ref, lq1, lk1, lq2, lk2, g_ref = refs[:6]
    kv = refs[6:6 + 2 * n_sources]
    o_ref, s0_ref, s1_ref, mx0_ref, mx1_ref, acc_ref = refs[6 + 2 * n_sources:]
    sources = [(kv[2 * s], kv[2 * s + 1]) for s in range(n_sources)]
    n_sub = qt_ref.shape[1] // tq
    half = HEAD_DIM // 2
    lam = (jnp.exp(jnp.sum(lq1[...] * lk1[...], axis=-1, keepdims=True))
           - jnp.exp(jnp.sum(lq2[...] * lk2[...], axis=-1, keepdims=True)) + lam_init)
    dim = lax.broadcasted_iota(jnp.int32, (HEAD_DIM, tq), 0)
    q_lists = []
    for j in range(n_sub):
        q_list = []
        for h in range(hps):
            qt = qt_ref[h * HEAD_DIM:(h + 1) * HEAD_DIM, j * tq:(j + 1) * tq]
            zero = jnp.zeros_like(qt)
            q_list.append((jnp.where(dim < half, qt, zero), h, h))
            q_list.append((jnp.where(dim < half, zero, qt), h, h))
        q_lists.append(q_list)

    def finish(j, outs):
        for h in range(hps):
            ot = outs[2 * h] - lam * outs[2 * h + 1]
            ot = ot * lax.rsqrt(jnp.mean(ot * ot, axis=0, keepdims=True) + EPS) * g_ref[...]
            o_ref[j * tq:(j + 1) * tq, h * HEAD_DIM:(h + 1) * HEAD_DIM] = (ot * (1.0 - lam_init)).T.astype(BF16)

    _attn_chains(q_lists, sources, tq, tk, (s0_ref, s1_ref), (mx0_ref, mx1_ref), acc_ref, finish)


def _diff_attention(qt_arr, kv_arrs, lams, g_diff, *, batch, sq, heads, qt_unit, k_col, vt_unit,
                    hps, tq, tq_sub, tk, lam_init, name):
    nq = sq // tq
    assert tq % tq_sub == 0
    half = HEAD_DIM // 2
    w = hps * HEAD_DIM
    assert heads % hps == 0 and qt_unit % hps == 0 and k_col % hps == 0 and vt_unit % hps == 0
    small = lambda width: pl.BlockSpec((1, width), lambda b, h, i: (0, 0))
    in_specs = [pl.BlockSpec((None, w, tq), lambda b, h, i: (b, qt_unit // hps + h, i)),
                small(half), small(half), small(half), small(half),
                pl.BlockSpec((HEAD_DIM, 1), lambda b, h, i: (0, 0))]
    args = [qt_arr, *lams, g_diff.reshape(HEAD_DIM, 1)]
    for k_arr, vt_arr, t in kv_arrs:
        in_specs.append(pl.BlockSpec((t, w), lambda b, h, i: (b, k_col // hps + h)))
        in_specs.append(pl.BlockSpec((None, w, t), lambda b, h, i: (b, vt_unit // hps + h, 0)))
        args += [k_arr, vt_arr]
    kern = functools.partial(_diff_kernel, hps=hps, tq=tq_sub, tk=tk, lam_init=lam_init,
                             n_sources=len(kv_arrs))
    return pl.pallas_call(
        kern,
        grid=(batch, heads // hps, nq),
        in_specs=in_specs,
        out_specs=pl.BlockSpec((tq, w), lambda b, h, i: (b * nq + i, h)),
        out_shape=jax.ShapeDtypeStruct((batch * sq, heads * HEAD_DIM), BF16),
        scratch_shapes=_attn_scratch(2 * hps, tq_sub, tk),
        compiler_params=_cparams(("parallel", "parallel", "parallel")),
        name=name,
    )(*args)


def _na_plan(seq, tq, wr_max, win_rows):
    rows_n = seq // GRID_W
    wr = min(wr_max, rows_n)
    r_per = tq // GRID_W
    outside = 2 * wr_max - 1
    starts, pats = [], []
    for blk in range(seq // tq):
        rf = blk * r_per
        lo = int(np.clip(rf - wr // 2, 0, rows_n - wr))
        w0 = min(lo, rows_n - win_rows)
        r = rf + np.arange(r_per)
        kr = w0 + np.arange(win_rows)
        r0 = np.clip(r - wr // 2, 0, rows_n - wr)
        row_ok = (kr[None, :] >= r0[:, None]) & (kr[None, :] < r0[:, None] + wr)
        roff = kr[None, :] - r[:, None] + wr_max - 1
        starts.append(w0 * GRID_W)
        pats.append(np.where(row_ok, roff, outside))
    uniq, pids = [], []
    for p in pats:
        for j, u in enumerate(uniq):
            if np.array_equal(u, p):
                pids.append(j)
                break
        else:
            pids.append(len(uniq))
            uniq.append(p)
    return np.asarray(starts, np.int32), np.asarray(pids, np.int32), np.stack(uniq)


def _na_bias(rpb_l, row_pats, tq, wk):
    heads = rpb_l.shape[0]
    cidx = np.arange(GRID_W)
    c0 = np.clip(cidx - NA_COLS // 2, 0, GRID_W - NA_COLS)
    col_ok = (cidx[None, :] >= c0[:, None]) & (cidx[None, :] < c0[:, None] + NA_COLS)
    coff = np.clip(cidx[None, :] - cidx[:, None], -(NA_COLS - 1), NA_COLS - 1) + NA_COLS - 1
    n_coff = rpb_l.shape[2]
    onehot = jnp.asarray(coff.reshape(-1)[None, :] == np.arange(n_coff)[:, None], F32)
    picked = jnp.dot(rpb_l.astype(F32).reshape(-1, n_coff), onehot, precision=lax.Precision.HIGHEST)
    picked = picked.reshape(heads, rpb_l.shape[1], GRID_W, GRID_W)
    by_col = jnp.where(col_ok[None, None], picked * LOG2E, NEG_BIG)
    outside = jnp.full((heads, GRID_W, GRID_W), NEG_BIG, F32)
    n_rel = by_col.shape[1]
    tile = lambda rel: by_col[:, rel] if rel < n_rel else outside
    strips = [jnp.concatenate([tile(int(rel)) for rel in q_row], axis=-1)
              for pat in row_pats for q_row in pat]
    n_pat = row_pats.shape[0]
    return jnp.stack(strips, axis=1).reshape(heads, n_pat, tq, wk)


def _na_kernel(w0_ref, pid_ref, q_ref, kc_ref, vc_ref, k_ref, v_ref, *rest, heads, tq, wk):
    del pid_ref
    bias_refs, o_ref = rest[:-1], rest[-1]
    n_sub = len(bias_refs)
    nt = (((1,), (1,)), ((), ()))
    units = []
    for sub in range(n_sub):
        w0 = pl.multiple_of(w0_ref[n_sub * pl.program_id(1) + sub], GRID_W)
        for h in range(heads):
            units.append((slice(sub * tq, (sub + 1) * tq), slice(h * HEAD_DIM, (h + 1) * HEAD_DIM),
                          pl.ds(w0, wk), bias_refs[sub], h))
    scores = []
    for rows, lanes, win, bias_ref, h in units:
        q = q_ref[rows, lanes]
        s_w = lax.dot_general(q, k_ref[win, lanes], nt, preferred_element_type=F32) + bias_ref[h]
        s_c = lax.dot_general(q, kc_ref[:, lanes], nt, preferred_element_type=F32)
        scores.append((s_w, s_c))
    probs = []
    for s_w, s_c in scores:
        m = jnp.maximum(jnp.max(s_w, axis=-1, keepdims=True), jnp.max(s_c, axis=-1, keepdims=True))
        p_w = jnp.exp2(s_w - m)
        p_c = jnp.exp2(s_c - m)
        l = jnp.sum(p_w, axis=-1, keepdims=True) + jnp.sum(p_c, axis=-1, keepdims=True)
        probs.append((p_w.astype(BF16), p_c.astype(BF16), l))
    for (p_w, p_c, l), (rows, lanes, win, _, _) in zip(probs, units):
        acc = (jnp.dot(p_w, v_ref[win, lanes], preferred_element_type=F32)
               + jnp.dot(p_c, vc_ref[:, lanes], preferred_element_type=F32))
        o_ref[rows, lanes] = (acc / l).astype(BF16)


def _na_attention(qkv, qkv_c, rpb_l, *, batch, seq, ctx_len, heads, q_col, k_col, v_col, tq):
    wr_max = (rpb_l.shape[1] + 1) // 2
    r_per = tq // GRID_W
    win_rows = min(-(-(r_per + wr_max - 1) // 4) * 4, seq // GRID_W)
    wk = win_rows * GRID_W
    starts, pids, row_pats = _na_plan(seq, tq, wr_max, win_rows)
    bias = _na_bias(rpb_l, row_pats, tq, wk)
    nq = seq // tq
    n_sub = 2 if nq % 2 == 0 else 1
    ns = nq // n_sub
    w = heads * HEAD_DIM
    kern = functools.partial(_na_kernel, heads=heads, tq=tq, wk=wk)
    bias_spec = lambda sub: pl.BlockSpec((heads, None, tq, wk),
                                         lambda b, i, w0, pid: (0, pid[n_sub * i + sub], 0, 0))
    grid_spec = pltpu.PrefetchScalarGridSpec(
        num_scalar_prefetch=2,
        grid=(batch, ns),
        in_specs=[
            pl.BlockSpec((n_sub * tq, w), lambda b, i, w0, pid: (b * ns + i, q_col // heads)),
            pl.BlockSpec((ctx_len, w), lambda b, i, w0, pid: (b, k_col // heads)),
            pl.BlockSpec((ctx_len, w), lambda b, i, w0, pid: (b, v_col // heads)),
            pl.BlockSpec((seq, w), lambda b, i, w0, pid: (b, k_col // heads)),
            pl.BlockSpec((seq, w), lambda b, i, w0, pid: (b, v_col // heads)),
        ] + [bias_spec(sub) for sub in range(n_sub)],
        out_specs=pl.BlockSpec((n_sub * tq, w), lambda b, i, w0, pid: (b * ns + i, 0)),
    )
    return pl.pallas_call(
        kern,
        grid_spec=grid_spec,
        out_shape=jax.ShapeDtypeStruct((batch * seq, w), BF16),
        compiler_params=_cparams(("parallel", "parallel")),
        name="na_attention",
    )(jnp.asarray(starts), jnp.asarray(pids), qkv, qkv_c, qkv_c, qkv, qkv, *([bias] * n_sub))


def _dense_ctx_kernel(q_ref, k_ref, v_ref, o_ref, *, heads):
    for h in range(heads):
        lanes = slice(h * HEAD_DIM, (h + 1) * HEAD_DIM)
        s = lax.dot_general(q_ref[:, lanes], k_ref[:, lanes], (((1,), (1,)), ((), ())),
                            preferred_element_type=F32)
        p = jnp.exp2(s - jnp.max(s, axis=-1, keepdims=True))
        acc = jnp.dot(p.astype(BF16), v_ref[:, lanes], preferred_element_type=F32)
        o_ref[:, lanes] = (acc / jnp.sum(p, axis=-1, keepdims=True)).astype(BF16)


def _dense_ctx_attention(qkv_c, *, batch, ctx_len, heads, q_col, k_col, v_col):
    w = heads * HEAD_DIM
    spec = lambda col: pl.BlockSpec((ctx_len, w), lambda b: (b, col // heads))
    return pl.pallas_call(
        functools.partial(_dense_ctx_kernel, heads=heads),
        grid=(batch,),
        in_specs=[spec(q_col), spec(k_col), spec(v_col)],
        out_specs=pl.BlockSpec((ctx_len, w), lambda b: (b, 0)),
        out_shape=jax.ShapeDtypeStruct((batch * ctx_len, w), BF16),
        compiler_params=_cparams(("parallel",)),
        name="dense_attention_ctx",
    )(qkv_c, qkv_c, qkv_c)


def _outproj_kernel(oa_ref, ob_ref, on_ref, w_ref, h_ref, mod_ref, g_ref, o_ref):
    wa, wb = oa_ref.shape[1], ob_ref.shape[1]
    tm = o_ref.shape[0]
    slab = min(tm, 2 * HEAD_DIM)
    for r0 in range(0, tm, slab):
        rows = slice(r0, r0 + slab)
        y = jnp.dot(oa_ref[rows, :], w_ref[:wa, :], preferred_element_type=F32)
        y += jnp.dot(ob_ref[rows, :], w_ref[wa:wa + wb, :], preferred_element_type=F32)
        y += jnp.dot(on_ref[rows, :], w_ref[wa + wb:, :], preferred_element_type=F32)
        yn = y * lax.rsqrt(jnp.mean(y * y, axis=-1, keepdims=True) + EPS) * g_ref[...]
        o_ref[rows, :] = h_ref[rows, :] + mod_ref[2:3] * yn


def _out_projection(oa, ob, on, w_bf16, h2d, mods, mod_row_of_block, g_post, *, layer, tm):
    rows, d = h2d.shape
    kdim = w_bf16.shape[1]
    return pl.pallas_call(
        _outproj_kernel,
        grid=(rows // tm,),
        in_specs=[
            pl.BlockSpec((tm, oa.shape[1]), lambda i: (i, 0)),
            pl.BlockSpec((tm, ob.shape[1]), lambda i: (i, 0)),
            pl.BlockSpec((tm, on.shape[1]), lambda i: (i, 0)),
            pl.BlockSpec((None, kdim, d), lambda i: (layer, 0, 0), pipeline_mode=pl.Buffered(1)),
            pl.BlockSpec((tm, d), lambda i: (i, 0)),
            pl.BlockSpec((None, 6, d), lambda i: (mod_row_of_block(i), 0, 0)),
            pl.BlockSpec((1, d), lambda i: (0, 0)),
        ],
        out_specs=pl.BlockSpec((tm, d), lambda i: (i, 0)),
        out_shape=jax.ShapeDtypeStruct((rows, d), F32),
        compiler_params=_cparams(("parallel",)),
        name="out_projection",
    )(oa, ob, on, w_bf16, h2d, mods, g_post)


def _ffn_kernel(h_ref, mod_ref, gpre_ref, wg_ref, wu_ref, wd_ref, gpost_ref, o_ref, xm_ref):
    f = pl.program_id(1)
    last = pl.num_programs(1) - 1
    tm = o_ref.shape[0]
    slab = min(tm, 2 * HEAD_DIM)

    def swiglu_down(xm):
        gate = jnp.dot(xm, wg_ref[...], preferred_element_type=F32)
        up = jnp.dot(xm, wu_ref[...], preferred_element_type=F32)
        act = (gate * jax.nn.sigmoid(gate) * up).astype(BF16)
        return jnp.dot(act, wd_ref[...], preferred_element_type=F32)

    @pl.when(f == 0)
    def _():
        for r0 in range(0, tm, slab):
            rows = slice(r0, r0 + slab)
            ms = jnp.mean(jnp.square(h_ref[rows, :]), axis=-1, keepdims=True)
            xn = h_ref[rows, :] * lax.rsqrt(ms + EPS) * gpre_ref[...]
            xm = (xn * (1.0 + mod_ref[4:5]) + mod_ref[3:4]).astype(BF16)
            xm_ref[rows, :] = xm
            o_ref[rows, :] = swiglu_down(xm)

    @pl.when(jnp.logical_and(f > 0, f < last))
    def _():
        o_ref[...] += swiglu_down(xm_ref[...])

    @pl.when(f == last)
    def _():
        for r0 in range(0, tm, slab):
            rows = slice(r0, r0 + slab)
            o_ref[rows, :] += swiglu_down(xm_ref[rows, :])
            ms = jnp.mean(jnp.square(o_ref[rows, :]), axis=-1, keepdims=True)
            yn = o_ref[rows, :] * lax.rsqrt(ms + EPS) * gpost_ref[...]
            o_ref[rows, :] = h_ref[rows, :] + mod_ref[5:6] * yn


def _ffn(h2d, mods, mod_row_of_block, g_pre, wg, wu, wd, g_post, *, layer, tm, tf):
    rows, d = h2d.shape
    ff = wg.shape[2]
    assert ff // tf >= 2
    return pl.pallas_call(
        _ffn_kernel,
        grid=(rows // tm, ff // tf),
        in_specs=[
            pl.BlockSpec((tm, d), lambda i, f: (i, 0)),
            pl.BlockSpec((None, 6, d), lambda i, f: (mod_row_of_block(i), 0, 0)),
            pl.BlockSpec((1, d), lambda i, f: (0, 0)),
            pl.BlockSpec((None, d, tf), lambda i, f: (layer, 0, f)),
            pl.BlockSpec((None, d, tf), lambda i, f: (layer, 0, f)),
            pl.BlockSpec((None, tf, d), lambda i, f: (layer, f, 0)),
            pl.BlockSpec((1, d), lambda i, f: (0, 0)),
        ],
        out_specs=pl.BlockSpec((tm, d), lambda i, f: (i, 0)),
        out_shape=jax.ShapeDtypeStruct((rows, d), F32),
        scratch_shapes=[pltpu.VMEM((tm, d), BF16)],
        compiler_params=_cparams(("parallel", "arbitrary")),
        name="ffn",
    )(h2d, mods, g_pre, wg, wu, wd, g_post)


def kernel(x, c, ctx, c_ctx, w_mod, b_mod, g_pre1, g_post1, g_pre2, g_post2, w_in, w_out,
           lam_q1, lam_k1, lam_q2, lam_k2, g_diff, g_qn, g_kn, rpb, w_gate, w_up, w_down):
    batch, seq, d = x.shape
    ctx_len = ctx.shape[1]
    depth = w_mod.shape[0]
    n_heads = d // HEAD_DIM
    a_heads, b_heads, c_heads = n_heads // 4, n_heads // 2, n_heads // 4
    b_kv = b_heads // 4
    group = b_heads // b_kv
    a_qk = a_v = a_heads * HEAD_DIM
    b_q, b_kvw, c_w = b_heads * HEAD_DIM, b_kv * HEAD_DIM, c_heads * HEAD_DIM
    sizes = (a_qk, a_qk, a_v, b_q, b_kvw, b_kvw, c_w, c_w, c_w)
    starts = np.cumsum((0,) + sizes)[:-1] // HEAD_DIM
    _, ka_c, _, _, kb_c, _, qn_c, kn_c, vn_c = (int(s) for s in starts)
    qta_u, qtb_u = 0, a_heads
    vta_u, vtb_u = a_heads + b_heads, 2 * a_heads + b_heads
    assert batch + 1 <= MOD_ROWS and seq % GRID_W == 0

    tm = _pick(seq, (512, 256, 128))
    tm_c = _pick(ctx_len, (512, 256, 128))
    tq = _pick(seq, (256, 128))
    tq_dense = _pick(seq, (2048, 1024, 512, 256, 128))
    tq_sub = min(tq_dense, 1024)
    tq_c = _pick(ctx_len, (256, 128))
    tk = 512
    tf = _pick(w_gate.shape[2], (512, 256, 128))
    tm_ffn = _pick(seq, (1024, 512, 256, 128))
    tm_ffn_c = _pick(batch * ctx_len, (1024, 512, 256, 128))
    hps = 2 if a_heads % 2 == 0 else 1

    c_rows = jnp.zeros((MOD_ROWS, d), F32).at[:batch].set(c).at[batch].set(c_ctx)
    mods = _mod_vectors(c_rows, w_mod, b_mod).reshape(depth, MOD_ROWS, 6, d)
    tabs_a = _rope_tables(seq, HEAD_DIM // 2)
    tabs_b = _rope_tables(seq, HEAD_DIM)
    tabs_c = jnp.zeros((3, tm_c, HEAD_DIM), F32)

    lat_row = lambda i: i // (seq // tm)
    ctx_row = lambda i: batch
    row1 = lambda v: v.reshape(1, -1)

    w_in_b, w_out_b = _to_bf16(w_in), _to_bf16(w_out)
    wg, wu, wd = _to_bf16(w_gate), _to_bf16(w_up), _to_bf16(w_down)

    h = x.reshape(batch * seq, d)
    hc = ctx.reshape(batch * ctx_len, d)
    for l in range(depth):
        need_ctx = l < depth - 1
        lam_init = 0.8 - 0.6 * math.exp(-0.3 * l)
        lams = [row1(v[l]) for v in (lam_q1, lam_k1, lam_q2, lam_k2)]
        inproj = functools.partial(_in_projection, g_pre=row1(g_pre1[l]), w_bf16=w_in_b, layer=l,
                                   g_qn=row1(g_qn[l]), g_kn=row1(g_kn[l]), sizes=sizes, batch=batch)
        qkv, vt = inproj(h, mods[l], lat_row, tabs_a=tabs_a, tabs_b=tabs_b, seq=seq, tm=tm, use_rope=True)
        qkv_c, vt_c = inproj(hc, mods[l], ctx_row, tabs_a=tabs_c, tabs_b=tabs_c, seq=ctx_len, tm=tm_c,
                             use_rope=False)
        both = [(qkv_c, vt_c, ctx_len), (qkv, vt, seq)]
        only_ctx = [(qkv_c, vt_c, ctx_len)]

        oa = _diff_attention(vt, both, lams, row1(g_diff[l]), batch=batch, sq=seq, heads=a_heads,
                             qt_unit=qta_u, k_col=ka_c, vt_unit=vta_u, hps=hps, tq=tq_dense,
                             tq_sub=tq_sub, tk=tk,
                             lam_init=lam_init, name="diff_attention")
        ob = _gqa_attention(vt, qtb_u, both, kb_c, vtb_u, batch=batch, sq=seq, n_kv=b_kv, group=group,
                            tq=tq_dense, tq_sub=tq_sub, tk=tk, name="gqa_attention")
        on = _na_attention(qkv, qkv_c, rpb[l], batch=batch, seq=seq, ctx_len=ctx_len, heads=c_heads,
                           q_col=qn_c, k_col=kn_c, v_col=vn_c, tq=tq)
        h_mid = _out_projection(oa, ob, on, w_out_b, h, mods[l], lambda i: i // (seq // tm_ffn),
                                row1(g_post1[l]), layer=l, tm=tm_ffn)
        h_new = _ffn(h_mid, mods[l], lambda i: i // (seq // tm_ffn), row1(g_pre2[l]), wg, wu, wd,
                     row1(g_post2[l]), layer=l, tm=tm_ffn, tf=tf)

        if need_ctx:
            oac = _diff_attention(vt_c, only_ctx, lams, row1(g_diff[l]), batch=batch, sq=ctx_len,
                                  heads=a_heads, qt_unit=qta_u, k_col=ka_c, vt_unit=vta_u, hps=hps,
                                  tq=tq_c, tq_sub=tq_c, tk=tk, lam_init=lam_init,
                                  name="diff_attention_ctx")
            obc = _gqa_attention(vt_c, qtb_u, only_ctx, kb_c, vtb_u, batch=batch, sq=ctx_len, n_kv=b_kv,
                                 group=group, tq=tq_c, tq_sub=tq_c, tk=tk, name="gqa_attention_ctx")
            onc = _dense_ctx_attention(qkv_c, batch=batch, ctx_len=ctx_len, heads=c_heads, q_col=qn_c,
                                       k_col=kn_c, v_col=vn_c)
            hc_mid = _out_projection(oac, obc, onc, w_out_b, hc, mods[l], ctx_row, row1(g_post1[l]),
                                     layer=l, tm=tm_c)
            hc = _ffn(hc_mid, mods[l], ctx_row, row1(g_pre2[l]), wg, wu, wd, row1(g_post2[l]),
                      layer=l, tm=tm_ffn_c, tf=tf)
        h = h_new
    return h.reshape(batch, seq, d)
```
